```python
import math
import jax
import jax.numpy as jnp
from jax import lax
import numpy as np

D_MODEL = 2048
BATCH = 2
SEQ = 4096
DEPTH = 4
DEC_BATCH = 8
DEC_SEQ = 1
PAST_LEN = 16384
PAGE_SIZE = 128

N_EVEN = (DEPTH + 1) // 2
N_ODD = DEPTH // 2
A_HEADS = 8
A_QK_DIM = 64
A_V_DIM = 2 * A_QK_DIM
B_HEADS = 8
B_KV_HEADS = 2
B_HEAD_DIM = 128
CMP_BLOCK = 64
SEL_BLOCK = 64
TOP_N = 16
N_LOCAL = 2
WINDOW = 512
FORCED_SCORE = 1.0e4
C_HEADS = 16
C_HEAD_DIM = 128
D_FF = 5632
N_EXPERTS = 8
TOP_K = 2
D_FF_EXPERT = 5632
ROPE_THETA = 10000.0
Q_BLOCK = 128
LN_EPS = 1e-5
RMS_EPS = 1e-5
ALPHA = (2 * DEPTH) ** 0.25
BETA = (8 * DEPTH) ** -0.25
EVEN_SPLITS = (A_HEADS * 2 * A_QK_DIM, A_HEADS * 2 * A_QK_DIM, A_HEADS * A_V_DIM, B_HEADS * B_HEAD_DIM,
               B_KV_HEADS * B_HEAD_DIM, B_KV_HEADS * B_HEAD_DIM, B_KV_HEADS * B_HEAD_DIM,
               B_KV_HEADS * B_HEAD_DIM, B_KV_HEADS * B_HEAD_DIM, B_KV_HEADS * B_HEAD_DIM, 3 * B_HEADS)
EVEN_IN = sum(EVEN_SPLITS)
EVEN_MIX = A_HEADS * A_V_DIM + B_HEADS * B_HEAD_DIM
ODD_IN = 3 * C_HEADS * C_HEAD_DIM
ODD_MIX = C_HEADS * C_HEAD_DIM

kernel_name = 'hybrid_diff_nsa_stickbreak_decoder_step'


def split_cols(x, sizes):
    offs = np.cumsum(sizes)[:-1].tolist()
    return jnp.split(x, offs, axis=-1)


def layer_norm(x, g, b):
    xf = x.astype(jnp.float32)
    mu = jnp.mean(xf, axis=-1, keepdims=True)
    var = jnp.mean(jnp.square(xf - mu), axis=-1, keepdims=True)
    return ((xf - mu) * lax.rsqrt(var + LN_EPS) * g + b).astype(x.dtype)


def apply_rope(x, pos):
    d = x.shape[-1]
    half = d // 2
    inv_freq = ROPE_THETA ** (-jnp.arange(half, dtype=jnp.float32) / half)
    ang = pos.astype(jnp.float32)[:, None] * inv_freq[None, :]
    cos = jnp.cos(ang)[:, None, :]
    sin = jnp.sin(ang)[:, None, :]
    xs = x.reshape(x.shape[:2] + (-1, d)).astype(jnp.float32)
    x1, x2 = xs[..., :half], xs[..., half:]
    out = jnp.concatenate([x1 * cos - x2 * sin, x2 * cos + x1 * sin], axis=-1)
    return out.reshape(x.shape).astype(x.dtype)


def gather_pages(pool, page_table):
    rows = pool[page_table]
    return rows.reshape((page_table.shape[0], page_table.shape[1] * pool.shape[1]) + pool.shape[2:])


def sweep_query_blocks(block_fn, n_q):
    qb = min(Q_BLOCK, n_q)
    starts = jnp.arange(n_q // qb, dtype=jnp.int32) * qb
    outs = lax.map(lambda s: block_fn(s, qb), starts)
    return jax.tree_util.tree_map(
        lambda o: jnp.moveaxis(o, 0, 1).reshape((o.shape[1], n_q) + o.shape[3:]), outs)


def diff_attention(q, k, v, lam, q_pos0):
    n_q = q.shape[1]
    kpos = jnp.arange(k.shape[1], dtype=jnp.int32)
    scale = A_QK_DIM ** -0.5

    def block(s, qb):
        qblk = lax.dynamic_slice_in_dim(q, s, qb, axis=1)
        qpos = q_pos0 + s + jnp.arange(qb, dtype=jnp.int32)
        sc = jnp.einsum('bqhmd,bkhmd->bhmqk', qblk, k).astype(jnp.float32) * scale
        sc = jnp.where(kpos[None, :] <= qpos[:, None], sc, -jnp.inf)
        p = jax.nn.softmax(sc, axis=-1)
        w = p[:, :, 0] - lam * p[:, :, 1]
        return jnp.einsum('bhqk,bkhd->bqhd', w.astype(v.dtype), v)

    return sweep_query_blocks(block, n_q)


def nsa_attention(q_raw, q_rot, gates, kv_all, win, w_off, q_pos0, cmp_pe, cmp_w):
    bsz, n_q = q_raw.shape[:2]
    seq_len = kv_all.shape[1]
    n_grp = B_KV_HEADS
    rep = B_HEADS // B_KV_HEADS
    dh = B_HEAD_DIM
    scale = dh ** -0.5
    qpos_all = q_pos0 + jnp.arange(n_q, dtype=jnp.int32)

    nb_c = seq_len // CMP_BLOCK
    blocks = kv_all[:, :nb_c * CMP_BLOCK, 0:2].reshape(bsz, nb_c, CMP_BLOCK, 2, n_grp, dh)
    summ = jnp.mean(blocks + jnp.transpose(cmp_pe, (1, 0, 2))[:, :, None, :], axis=2)
    summ = jnp.einsum('bnjgd,jde->bnjge', summ, cmp_w)
    k_cmp, v_cmp = summ[:, :, 0], summ[:, :, 1]
    qg_raw = q_raw.reshape(bsz, n_q, n_grp, rep, dh)
    sc = jnp.einsum('btgrd,bngd->bgrtn', qg_raw, k_cmp).astype(jnp.float32) * scale
    cvalid = (jnp.arange(nb_c, dtype=jnp.int32) + 1) * CMP_BLOCK <= (qpos_all + 1)[:, None]
    sc = jnp.where(cvalid, sc, -jnp.inf)
    m = jnp.max(sc, axis=-1, keepdims=True)
    m = jnp.where(jnp.isfinite(m), m, 0.0)
    e = jnp.exp(sc - m)
    p_cmp = e / jnp.maximum(jnp.sum(e, axis=-1, keepdims=True), 1e-30)
    o_cmp = jnp.einsum('bgrtn,bngd->btgrd', p_cmp, v_cmp.astype(jnp.float32)).reshape(bsz, n_q, B_HEADS, dh)

    nb_s = -(-seq_len // SEL_BLOCK)
    n_sel = min(TOP_N, nb_s)
    imp = jnp.pad(jnp.sum(p_cmp, axis=2), ((0, 0), (0, 0), (0, 0), (0, nb_s - nb_c)))
    blk = jnp.arange(nb_s, dtype=jnp.int32)[None, :]
    cur = (qpos_all // SEL_BLOCK)[:, None]
    forced = (blk == 0) | ((blk <= cur) & (blk > cur - N_LOCAL))
    score = jnp.where(blk > cur, -jnp.inf, jnp.where(forced, FORCED_SCORE, imp))
    _, sel_idx = lax.top_k(score, n_sel)

    sel = jnp.pad(kv_all[:, :, 2:4], ((0, 0), (0, nb_s * SEL_BLOCK - seq_len), (0, 0), (0, 0), (0, 0)))
    sel_blocks = jnp.transpose(sel.reshape(bsz, nb_s, SEL_BLOCK, 2, n_grp, dh), (0, 4, 1, 2, 3, 5))
    win_pad = jnp.pad(win, ((0, 0), (WINDOW, 0), (0, 0), (0, 0), (0, 0)))
    bi = jnp.arange(bsz)[:, None, None]
    gi = jnp.arange(n_grp)[None, :, None]

    def block(s, qb):
        qblk = lax.dynamic_slice_in_dim(q_rot, s, qb, axis=1).reshape(bsz, qb, n_grp, rep, dh)
        qpos = q_pos0 + s + jnp.arange(qb, dtype=jnp.int32)
        idx = lax.dynamic_slice_in_dim(sel_idx, s, qb, axis=2)
        gath = sel_blocks[bi, gi, idx.reshape(bsz, n_grp, qb * n_sel)]
        gath = gath.reshape(bsz, n_grp, qb, n_sel * SEL_BLOCK, 2, dh)
        kpos = (idx[..., None] * SEL_BLOCK + jnp.arange(SEL_BLOCK, dtype=jnp.int32)).reshape(bsz, n_grp, qb, n_sel * SEL_BLOCK)
        ssc = jnp.einsum('bqgrd,bgqkd->bgrqk', qblk, gath[..., 0, :]).astype(jnp.float32) * scale
        smask = (kpos <= qpos[None, None, :, None])[:, :, None]
        ps = jax.nn.softmax(jnp.where(smask, ssc, -jnp.inf), axis=-1)
        o_sel = jnp.einsum('bgrqk,bgqkd->bqgrd', ps.astype(gath.dtype), gath[..., 1, :])
        wb = lax.dynamic_slice_in_dim(win_pad, s + (q_pos0 - w_off), WINDOW + qb, axis=1)
        kp = q_pos0 + s - WINDOW + jnp.arange(WINDOW + qb, dtype=jnp.int32)
        wmask = (kp[None, :] <= qpos[:, None]) & (kp[None, :] > qpos[:, None] - WINDOW) & (kp[None, :] >= w_off)
        wsc = jnp.einsum('bqgrd,bkgd->bgrqk', qblk, wb[:, :, 0]).astype(jnp.float32) * scale
        pw = jax.nn.softmax(jnp.where(wmask, wsc, -jnp.inf), axis=-1)
        o_win = jnp.einsum('bgrqk,bkgd->bqgrd', pw.astype(wb.dtype), wb[:, :, 1])
        return o_sel.reshape(bsz, qb, B_HEADS, dh), o_win.reshape(bsz, qb, B_HEADS, dh)

    o_sel, o_win = sweep_query_blocks(block, n_q)
    return gates[..., 0:1] * o_cmp + gates[..., 1:2] * o_sel + gates[..., 2:3] * o_win


def stick_breaking(q, k, v, q_pos0):
    n_q = q.shape[1]
    kpos = jnp.arange(k.shape[1], dtype=jnp.int32)
    scale = C_HEAD_DIM ** -0.5

    def block(s, qb):
        qblk = lax.dynamic_slice_in_dim(q, s, qb, axis=1)
        qpos = q_pos0 + s + jnp.arange(qb, dtype=jnp.int32)
        z = jnp.einsum('bqhd,bkhd->bhqk', qblk, k).astype(jnp.float32) * scale
        mask = kpos[None, :] < qpos[:, None]
        log1m = jnp.where(mask, -jax.nn.softplus(z), 0.0)
        between = lax.cumsum(log1m, axis=3, reverse=True) - log1m
        w = jnp.where(mask, jnp.exp(jax.nn.log_sigmoid(z) + between), 0.0)
        return jnp.einsum('bhqk,bkhd->bqhd', w.astype(v.dtype), v)

    return sweep_query_blocks(block, n_q)


def even_mixer(x, past, q_pos0, w_in, w_out, lam_vec, subln_g, cmp_pe, cmp_w, lam_init):
    bsz, n_t, _ = x.shape
    pos = q_pos0 + jnp.arange(n_t, dtype=jnp.int32)
    proj = x @ w_in
    aq, ak, av, bq, bkc, bvc, bks, bvs, bkw, bvw, bg = split_cols(proj, EVEN_SPLITS)
    aq = apply_rope(aq.reshape(bsz, n_t, A_HEADS, 2, A_QK_DIM), pos)
    ak = apply_rope(ak.reshape(bsz, n_t, A_HEADS, 2, A_QK_DIM), pos)
    kv_shape = (bsz, n_t, B_KV_HEADS, B_HEAD_DIM)
    bq_raw = bq.reshape(bsz, n_t, B_HEADS, B_HEAD_DIM)
    new_diff = jnp.stack([ak.reshape(bsz, n_t, A_HEADS, 2 * A_QK_DIM), av.reshape(bsz, n_t, A_HEADS, A_V_DIM)], axis=2)
    new_nsa = jnp.stack([bkc.reshape(kv_shape), bvc.reshape(kv_shape),
                         apply_rope(bks.reshape(kv_shape), pos), bvs.reshape(kv_shape)], axis=2)
    new_win = jnp.stack([apply_rope(bkw.reshape(kv_shape), pos), bvw.reshape(kv_shape)], axis=2)
    if past is None:
        all_diff, all_nsa, win = new_diff, new_nsa, new_win
        keep = min(WINDOW, n_t)
        w_off = 0
    else:
        past_diff, past_nsa, past_win = past
        all_diff = jnp.concatenate([past_diff, new_diff], axis=1)
        all_nsa = jnp.concatenate([past_nsa, new_nsa], axis=1)
        win = jnp.concatenate([past_win, new_win], axis=1)
        keep = past_win.shape[1]
        w_off = q_pos0 - keep
    seq_len = all_diff.shape[1]
    lv = lam_vec.astype(jnp.float32)
    lam = jnp.exp(jnp.sum(lv[0] * lv[1])) - jnp.exp(jnp.sum(lv[2] * lv[3])) + lam_init
    a_o = diff_attention(aq, all_diff[:, :, 0].reshape(bsz, seq_len, A_HEADS, 2, A_QK_DIM), all_diff[:, :, 1], lam, q_pos0)
    a_f = a_o.astype(jnp.float32)
    a_f = a_f * lax.rsqrt(jnp.mean(jnp.square(a_f), axis=-1, keepdims=True) + RMS_EPS) * subln_g * (1.0 - lam_init)
    gates = jax.nn.sigmoid(bg.astype(jnp.float32)).reshape(bsz, n_t, B_HEADS, 3)
    b_o = nsa_attention(bq_raw, apply_rope(bq_raw, pos), gates, all_nsa, win, w_off, q_pos0, cmp_pe, cmp_w)
    mixed = jnp.concatenate([a_f.reshape(bsz, n_t, -1), b_o.reshape(bsz, n_t, -1)], axis=-1).astype(x.dtype)
    return mixed @ w_out, new_diff, new_nsa, win[:, win.shape[1] - keep:]


def odd_mixer(x, past, q_pos0, w_in, w_out):
    bsz, n_t, _ = x.shape
    q, k, v = jnp.split(x @ w_in, 3, axis=-1)
    hs = (bsz, n_t, C_HEADS, C_HEAD_DIM)
    q = q.reshape(hs)
    new_kv = jnp.stack([k.reshape(hs), v.reshape(hs)], axis=2)
    all_kv = new_kv if past is None else jnp.concatenate([past, new_kv], axis=1)
    o = stick_breaking(q, all_kv[:, :, 0], all_kv[:, :, 1], q_pos0)
    return o.reshape(bsz, n_t, -1) @ w_out, new_kv


def swiglu(h, w_gu, w_down):
    g, u = jnp.split(h @ w_gu, 2, axis=-1)
    return (jax.nn.silu(g) * u) @ w_down


def moe_swiglu(h, w_router, w_gu, w_down):
    logits = (h @ w_router).astype(jnp.float32)
    top_v, top_i = lax.top_k(logits, TOP_K)
    gate = jax.nn.softmax(top_v, axis=-1)
    comb = jnp.sum(gate[..., None] * jax.nn.one_hot(top_i, N_EXPERTS, dtype=jnp.float32), axis=-2)
    out = jnp.zeros(h.shape, jnp.float32)
    for e in range(N_EXPERTS):
        out = out + comb[..., e:e + 1] * swiglu(h, w_gu[e], w_down[e]).astype(jnp.float32)
    return out.astype(h.dtype)


def setup_inputs(seed: int = 0) -> dict:
    key = jax.random.key(seed)
    k = jax.random.split(key, 24)
    f32 = jnp.float32
    n_pages = PAST_LEN // PAGE_SIZE
    n_used = DEC_BATCH * n_pages
    n_pool = n_used + n_used // 4
    win_buf = min(WINDOW, PAST_LEN)

    def nrm(kk, shape, scale):
        return jax.random.normal(kk, shape, f32) * scale

    page_table = jax.random.permutation(k[6], n_pool)[:n_used].reshape(DEC_BATCH, n_pages).astype(jnp.int32)
    return {
        'x_prompt': jax.random.normal(k[0], (BATCH, SEQ, D_MODEL), f32),
        'x_sample': jax.random.normal(k[1], (DEC_BATCH, DEC_SEQ, D_MODEL), f32),
        'cache_diff_kv': jax.random.normal(k[2], (N_EVEN, n_pool, PAGE_SIZE, 2, A_HEADS, A_V_DIM), f32),
        'cache_nsa_kv': jax.random.normal(k[3], (N_EVEN, n_pool, PAGE_SIZE, 4, B_KV_HEADS, B_HEAD_DIM), f32),
        'state_nsa_win': jax.random.normal(k[4], (N_EVEN, DEC_BATCH, win_buf, 2, B_KV_HEADS, B_HEAD_DIM), f32),
        'cache_sb_kv': jax.random.normal(k[5], (N_ODD, n_pool, PAGE_SIZE, 2, C_HEADS, C_HEAD_DIM), f32),
        'page_table': page_table,
        'even_w_in': nrm(k[7], (N_EVEN, D_MODEL, EVEN_IN), D_MODEL ** -0.5),
        'even_w_out': nrm(k[8], (N_EVEN, EVEN_MIX, D_MODEL), EVEN_MIX ** -0.5 * BETA),
        'diff_lambda': nrm(k[9], (N_EVEN, 4, A_QK_DIM), 0.1),
        'diff_subln_g': 1.0 + nrm(k[10], (N_EVEN, A_V_DIM), 0.02),
        'nsa_cmp_pe': nrm(k[11], (N_EVEN, 2, CMP_BLOCK, B_HEAD_DIM), 0.1),
        'nsa_cmp_w': nrm(k[12], (N_EVEN, 2, B_HEAD_DIM, B_HEAD_DIM), B_HEAD_DIM ** -0.5),
        'odd_w_in': nrm(k[13], (N_ODD, D_MODEL, ODD_IN), D_MODEL ** -0.5),
        'odd_w_out': nrm(k[14], (N_ODD, ODD_MIX, D_MODEL), ODD_MIX ** -0.5 * BETA),
        'ln1_g': 1.0 + nrm(k[15], (DEPTH, D_MODEL), 0.02),
        'ln1_b': nrm(k[16], (DEPTH, D_MODEL), 0.02),
        'ln2_g': 1.0 + nrm(k[17], (DEPTH, D_MODEL), 0.02),
        'ln2_b': nrm(k[18], (DEPTH, D_MODEL), 0.02),
        'ffn_w_gu': nrm(k[19], (N_EVEN, D_MODEL, 2 * D_FF), D_MODEL ** -0.5),
        'ffn_w_down': nrm(k[20], (N_EVEN, D_FF, D_MODEL), D_FF ** -0.5 * BETA),
        'moe_router': nrm(k[21], (N_ODD, D_MODEL, N_EXPERTS), D_MODEL ** -0.5),
        'moe_w_gu': nrm(k[22], (N_ODD, N_EXPERTS, D_MODEL, 2 * D_FF_EXPERT), D_MODEL ** -0.5),
        'moe_w_down': nrm(k[23], (N_ODD, N_EXPERTS, D_FF_EXPERT, D_MODEL), D_FF_EXPERT ** -0.5 * BETA),
    }


def reference(x_prompt, x_sample, cache_diff_kv, cache_nsa_kv, state_nsa_win, cache_sb_kv, page_table,
              even_w_in, even_w_out, diff_lambda, diff_subln_g, nsa_cmp_pe, nsa_cmp_w,
              odd_w_in, odd_w_out, ln1_g, ln1_b, ln2_g, ln2_b,
              ffn_w_gu, ffn_w_down, moe_router, moe_w_gu, moe_w_down):
    past_len = page_table.shape[1] * PAGE_SIZE
    xp, xs = x_prompt, x_sample
    diff_p, diff_s, nsa_p, nsa_s, win_p, win_s, sb_p, sb_s = [], [], [], [], [], [], [], []
    for layer in range(DEPTH):
        i = layer // 2
        if layer % 2 == 0:
            lam_init = 0.8 - 0.6 * math.exp(-0.3 * layer)
            prm = (even_w_in[i], even_w_out[i], diff_lambda[i], diff_subln_g[i], nsa_cmp_pe[i], nsa_cmp_w[i], lam_init)
            mp, dkp, nkp, wp = even_mixer(xp, None, 0, *prm)
            past = (gather_pages(cache_diff_kv[i], page_table), gather_pages(cache_nsa_kv[i], page_table), state_nsa_win[i])
            ms, dks, nks, wsn = even_mixer(xs, past, past_len, *prm)
            diff_p.append(dkp)
            diff_s.append(dks)
            nsa_p.append(nkp)
            nsa_s.append(nks)
            win_p.append(wp)
            win_s.append(wsn)
        else:
            mp, skp = odd_mixer(xp, None, 0, odd_w_in[i], odd_w_out[i])
            ms, sks = odd_mixer(xs, gather_pages(cache_sb_kv[i], page_table), past_len, odd_w_in[i], odd_w_out[i])
            sb_p.append(skp)
            sb_s.append(sks)
        xp = layer_norm(ALPHA * xp + mp, ln1_g[layer], ln1_b[layer])
        xs = layer_norm(ALPHA * xs + ms, ln1_g[layer], ln1_b[layer])
        if layer % 2 == 0:
            fp = swiglu(xp, ffn_w_gu[i], ffn_w_down[i])
            fs = swiglu(xs, ffn_w_gu[i], ffn_w_down[i])
        else:
            fp = moe_swiglu(xp, moe_router[i], moe_w_gu[i], moe_w_down[i])
            fs = moe_swiglu(xs, moe_router[i], moe_w_gu[i], moe_w_down[i])
        xp = layer_norm(ALPHA * xp + fp, ln2_g[layer], ln2_b[layer])
        xs = layer_norm(ALPHA * xs + fs, ln2_g[layer], ln2_b[layer])
    return (xp, xs, jnp.stack(diff_p), jnp.stack(diff_s), jnp.stack(nsa_p), jnp.stack(nsa_s),
            jnp.stack(win_p), jnp.stack(win_s), jnp.stack(sb_p), jnp.stack(sb_s))
```

```python
import functools
import math
from typing import NamedTuple

import numpy as np
import jax
import jax.numpy as jnp
from jax import lax
from jax.experimental import pallas as pl
from jax.experimental.pallas import tpu as pltpu

F32 = jnp.float32
BF16 = jnp.bfloat16

LANES = 128
VMEM_LIMIT = 56 * 1024 * 1024
NEG = -1.0e30


class Cfg(NamedTuple):
    d_model: int = 2048
    depth: int = 4
    page: int = 128
    a_heads: int = 8
    a_qk: int = 64
    b_heads: int = 8
    b_kv: int = 2
    cmp_block: int = 64
    top_n: int = 16
    n_local: int = 2
    window: int = 512
    forced: float = 1.0e4
    c_heads: int = 16
    d_ff: int = 5632
    n_exp: int = 8
    top_k: int = 2
    theta: float = 10000.0
    q_block: int = 128
    ln_eps: float = 1e-5
    rms_eps: float = 1e-5

    @property
    def alpha(self):
        return (2 * self.depth) ** 0.25

    @property
    def even_main(self):
        return 4 * self.a_heads * LANES + 6 * self.b_kv * LANES


def _cp(sem, vmem=VMEM_LIMIT):
    return pltpu.CompilerParams(dimension_semantics=sem, vmem_limit_bytes=vmem)


def _pick(n, pref):
    if n <= pref:
        return n
    t = pref
    while n % t:
        t //= 2
    return t


def _mm_kernel(x_ref, w_ref, o_ref, acc_ref, *, nk):
    part = jnp.dot(x_ref[...].astype(BF16), w_ref[...].astype(BF16), preferred_element_type=F32)
    if nk == 1:
        o_ref[...] = part.astype(o_ref.dtype)
        return
    k = pl.program_id(2)

    @pl.when(k == 0)
    def _():
        acc_ref[...] = part

    @pl.when(k > 0)
    def _():
        acc_ref[...] += part

    @pl.when(k == nk - 1)
    def _():
        o_ref[...] = acc_ref[...].astype(o_ref.dtype)


def _w_spec(w, lead, tk, tn, cb0):
    nlead = len(lead)
    return pl.BlockSpec((None,) * nlead + (tk, tn), lambda i, j, k: tuple(lead) + (k, cb0 + j))


def matmul(x, w, lead=(), col0=0, n=None, tm=1024, tn=512, tk=None, out_dtype=F32):
    m, kdim = x.shape
    n = w.shape[-1] - col0 if n is None else n
    tm = _pick(m, tm)
    tn = _pick(n, tn)
    tk = kdim if tk is None else _pick(kdim, tk)
    assert col0 % tn == 0 and m % tm == 0 and n % tn == 0 and kdim % tk == 0
    nk = kdim // tk
    return pl.pallas_call(
        functools.partial(_mm_kernel, nk=nk),
        grid=(m // tm, n // tn, nk),
        in_specs=[pl.BlockSpec((tm, tk), lambda i, j, k: (i, k)),
                  _w_spec(w, lead, tk, tn, col0 // tn)],
        out_specs=pl.BlockSpec((tm, tn), lambda i, j, k: (i, j)),
        out_shape=jax.ShapeDtypeStruct((m, n), out_dtype),
        scratch_shapes=[pltpu.VMEM((tm, tn), F32)],
        compiler_params=_cp(("parallel", "parallel", "arbitrary")),
        name="matmul",
    )(x, w)


def _swiglu_kernel(x_ref, wg_ref, wu_ref, o_ref):
    x = x_ref[...].astype(BF16)
    g = jnp.dot(x, wg_ref[...].astype(BF16), preferred_element_type=F32)
    u = jnp.dot(x, wu_ref[...].astype(BF16), preferred_element_type=F32)
    o_ref[...] = (g * jax.nn.sigmoid(g) * u).astype(o_ref.dtype)


def swiglu_up(x, w_gu, lead, d_ff, tm=1024, tn=512):
    m, kdim = x.shape
    tm = _pick(m, tm)
    tn = _pick(d_ff, tn)
    nb = d_ff // tn
    nlead = len(lead)
    wblk = (None,) * nlead + (kdim, tn)
    return pl.pallas_call(
        _swiglu_kernel,
        grid=(m // tm, nb),
        in_specs=[pl.BlockSpec((tm, kdim), lambda i, j: (i, 0)),
                  pl.BlockSpec(wblk, lambda i, j: tuple(lead) + (0, j)),
                  pl.BlockSpec(wblk, lambda i, j: tuple(lead) + (0, nb + j))],
        out_specs=pl.BlockSpec((tm, tn), lambda i, j: (i, j)),
        out_shape=jax.ShapeDtypeStruct((m, d_ff), BF16),
        compiler_params=_cp(("parallel", "parallel")),
        name="swiglu_up",
    )(x, w_gu, w_gu)


def _ln_kernel(x_ref, f_ref, g_ref, b_ref, o_ref, *, alpha, eps):
    y = alpha * x_ref[...] + f_ref[...].astype(F32)
    mu = jnp.mean(y, axis=-1, keepdims=True)
    d = y - mu
    var = jnp.mean(d * d, axis=-1, keepdims=True)
    o_ref[...] = d * lax.rsqrt(var + eps) * g_ref[...] + b_ref[...]


def add_ln(cfg, x, f, g, b, layer, tm=256):
    m, d = x.shape
    tm = _pick(m, tm)
    return pl.pallas_call(
        functools.partial(_ln_kernel, alpha=cfg.alpha, eps=cfg.ln_eps),
        grid=(m // tm,),
        in_specs=[pl.BlockSpec((tm, d), lambda i: (i, 0)),
                  pl.BlockSpec((tm, d), lambda i: (i, 0)),
                  pl.BlockSpec((None, 1, d), lambda i: (layer, 0, 0)),
                  pl.BlockSpec((None, 1, d), lambda i: (layer, 0, 0))],
        out_specs=pl.BlockSpec((tm, d), lambda i: (i, 0)),
        out_shape=jax.ShapeDtypeStruct((m, d), F32),
        compiler_params=_cp(("parallel",)),
        name="add_ln",
    )(x, f, g.reshape(-1, 1, d), b.reshape(-1, 1, d))


def rope_tables(cfg, pos):
    out = []
    for d in (cfg.a_qk, LANES):
        half = d // 2
        inv = cfg.theta ** (-jnp.arange(half, dtype=F32) / half)
        ang = pos.astype(F32)[:, None] * inv[None, :]
        c, s = jnp.cos(ang), jnp.sin(ang)
        reps = LANES // d
        out.append(jnp.tile(jnp.concatenate([c, c], axis=-1), (1, reps)))
        out.append(jnp.tile(jnp.concatenate([-s, s], axis=-1), (1, reps)))
    return out


def _rope(x, c, s, d):
    if d == LANES:
        partner = pltpu.roll(x, LANES // 2, 1)
    else:
        lane = lax.broadcasted_iota(jnp.int32, x.shape, 1)
        lo = (lane % d) < (d // 2)
        partner = jnp.where(lo, pltpu.roll(x, LANES - d // 2, 1), pltpu.roll(x, d // 2, 1))
    return x * c + partner * s


def _split_kernel(p_ref, c64_ref, s64_ref, c128_ref, s128_ref,
                  aq_ref, dkv_ref, bqr_ref, bqt_ref, nkv_ref, win_ref, *, cfg):
    c64, s64, c128, s128 = c64_ref[...], s64_ref[...], c128_ref[...], s128_ref[...]
    ah, g = cfg.a_heads, cfg.b_kv
    W = LANES

    def col(i):
        return p_ref[:, i * W:(i + 1) * W]

    for h in range(ah):
        aq_ref[:, h * W:(h + 1) * W] = _rope(col(h), c64, s64, cfg.a_qk)
        dkv_ref[:, h * W:(h + 1) * W] = _rope(col(ah + h), c64, s64, cfg.a_qk)
        dkv_ref[:, (ah + h) * W:(ah + h + 1) * W] = col(2 * ah + h)
    for h in range(cfg.b_heads):
        x = col(3 * ah + h)
        bqr_ref[:, h * W:(h + 1) * W] = x
        bqt_ref[:, h * W:(h + 1) * W] = _rope(x, c128, s128, W)
    base = 3 * ah + cfg.b_heads
    for s in range(4):
        for j in range(g):
            x = col(base + s * g + j)
            nkv_ref[:, (s * g + j) * W:(s * g + j + 1) * W] = _rope(x, c128, s128, W) if s == 2 else x
    base += 4 * g
    for s in range(2):
        for j in range(g):
            x = col(base + s * g + j)
            win_ref[:, (s * g + j) * W:(s * g + j + 1) * W] = _rope(x, c128, s128, W) if s == 0 else x


def even_split(cfg, proj, tabs, n_pos):
    m = proj.shape[0]
    tm = _pick(n_pos, 256)
    npb = n_pos // tm
    ah, g, W = cfg.a_heads, cfg.b_kv, LANES
    widths = (ah * W, 2 * ah * W, cfg.b_heads * W, cfg.b_heads * W, 4 * g * W, 2 * g * W)
    tab_spec = pl.BlockSpec((tm, W), lambda i: (i % npb, 0))
    return pl.pallas_call(
        functools.partial(_split_kernel, cfg=cfg),
        grid=(m // tm,),
        in_specs=[pl.BlockSpec((tm, proj.shape[1]), lambda i: (i, 0))] + [tab_spec] * 4,
        out_specs=[pl.BlockSpec((tm, w), lambda i: (i, 0)) for w in widths],
        out_shape=[jax.ShapeDtypeStruct((m, w), F32) for w in widths],
        compiler_params=_cp(("parallel",)),
        name="even_split",
    )(proj, *tabs)


def _softmax_step(s, m_ref, l_ref, acc_ref, v):
    m_old = m_ref[...]
    m_new = jnp.maximum(m_old, jnp.max(s, axis=-1, keepdims=True))
    a = jnp.exp(m_old - m_new)
    p = jnp.exp(s - m_new)
    l_ref[...] = a * l_ref[...] + jnp.sum(p, axis=-1, keepdims=True)
    acc_ref[...] = a * acc_ref[...] + jnp.dot(p.astype(BF16), v, preferred_element_type=F32)
    m_ref[...] = m_new


def _dot_t(a, b):
    return lax.dot_general(a, b, (((1,), (1,)), ((), ())), preferred_element_type=F32)


def _diff_lambda(lam_ref, lam_init):
    lv = lam_ref[...]
    a = jnp.sum(lv[0:1] * lv[1:2], axis=-1, keepdims=True)
    b = jnp.sum(lv[2:3] * lv[3:4], axis=-1, keepdims=True)
    return jnp.exp(a) - jnp.exp(b) + lam_init


def _diff_finish(o1, o2, lam, g, lam_init, eps):
    o = o1 - lam * o2
    o = o * lax.rsqrt(jnp.mean(o * o, axis=-1, keepdims=True) + eps)
    return o * g * (1.0 - lam_init)


def _diff_prompt_kernel(q_ref, k_ref, v_ref, lam_ref, g_ref, o_ref, m_ref, l_ref, acc_ref,
                        *, tq, d_qk, lam_init, eps):
    i = pl.program_id(2)
    q = q_ref[...] * (d_qk ** -0.5)
    lane = lax.broadcasted_iota(jnp.int32, q.shape, 1)
    q2 = jnp.concatenate([jnp.where(lane < d_qk, q, 0.0), jnp.where(lane >= d_qk, q, 0.0)], axis=0).astype(BF16)
    m_ref[...] = jnp.full(m_ref.shape, NEG, F32)
    l_ref[...] = jnp.zeros(l_ref.shape, F32)
    acc_ref[...] = jnp.zeros(acc_ref.shape, F32)

    def tile(j, masked):
        k = k_ref[pl.ds(j * tq, tq), :].astype(BF16)
        v = v_ref[pl.ds(j * tq, tq), :].astype(BF16)
        s = _dot_t(q2, k)
        if masked:
            row = lax.broadcasted_iota(jnp.int32, s.shape, 0) % tq
            colk = lax.broadcasted_iota(jnp.int32, s.shape, 1)
            s = jnp.where(colk <= row, s, NEG)
        _softmax_step(s, m_ref, l_ref, acc_ref, v)

    def body(j, c):
        tile(j, False)
        return c

    lax.fori_loop(0, i, body, 0)
    tile(i, True)
    o = acc_ref[...] / l_ref[...]
    lam = _diff_lambda(lam_ref, lam_init)
    o_ref[...] = _diff_finish(o[:tq], o[tq:], lam, g_ref[...], lam_init, eps)


def diff_prompt(cfg, aq, dkv, lam_vec, subln_g, layer_i, lam_init, bsz, n_t, tq=256):
    tq = _pick(n_t, tq)
    nq = n_t // tq
    ah, W = cfg.a_heads, LANES
    return pl.pallas_call(
        functools.partial(_diff_prompt_kernel, tq=tq, d_qk=cfg.a_qk, lam_init=lam_init, eps=cfg.rms_eps),
        grid=(bsz, ah, nq),
        in_specs=[pl.BlockSpec((tq, W), lambda b, h, i: (b * nq + i, h)),
                  pl.BlockSpec((n_t, W), lambda b, h, i: (b, h)),
                  pl.BlockSpec((n_t, W), lambda b, h, i: (b, ah + h)),
                  pl.BlockSpec((None, 4, cfg.a_qk), lambda b, h, i: (layer_i, 0, 0)),
                  pl.BlockSpec((None, 1, W), lambda b, h, i: (layer_i, 0, 0))],
        out_specs=pl.BlockSpec((tq, W), lambda b, h, i: (b * nq + i, h)),
        out_shape=jax.ShapeDtypeStruct((bsz * n_t, ah * W), F32),
        scratch_shapes=[pltpu.VMEM((2 * tq, 1), F32), pltpu.VMEM((2 * tq, 1), F32), pltpu.VMEM((2 * tq, W), F32)],
        compiler_params=_cp(("parallel", "parallel", "arbitrary")),
        name="diff_prompt",
    )(aq, dkv, dkv, lam_vec, subln_g.reshape(-1, 1, W))


def _cmp_prep_kernel(kv_ref, pe_ref, w_ref, o_ref, *, blk, g):
    rows = kv_ref[...]
    tb = rows.shape[0] // blk
    r = lax.broadcasted_iota(jnp.int32, (tb, rows.shape[0]), 0)
    c = lax.broadcasted_iota(jnp.int32, (tb, rows.shape[0]), 1)
    pool = jnp.where(c // blk == r, 1.0 / blk, 0.0).astype(F32)
    mean = jnp.dot(pool, rows, preferred_element_type=F32, precision=lax.Precision.HIGHEST)
    for j in range(2):
        pe = jnp.mean(pe_ref[j], axis=0, keepdims=True)
        for gg in range(g):
            sl = slice((j * g + gg) * LANES, (j * g + gg + 1) * LANES)
            o_ref[:, sl] = jnp.dot((mean[:, sl] + pe).astype(BF16), w_ref[j].astype(BF16),
                                   preferred_element_type=F32)


def cmp_prep(cfg, nsa_kv, cmp_pe, cmp_w, layer_i):
    m = nsa_kv.shape[0]
    blk, g, W = cfg.cmp_block, cfg.b_kv, LANES
    nb = m // blk
    tb = _pick(nb, 8)
    return pl.pallas_call(
        functools.partial(_cmp_prep_kernel, blk=blk, g=g),
        grid=(nb // tb,),
        in_specs=[pl.BlockSpec((tb * blk, 2 * g * W), lambda i: (i, 0)),
                  pl.BlockSpec((None, 2, blk, W), lambda i: (layer_i, 0, 0, 0)),
                  pl.BlockSpec((None, 2, W, W), lambda i: (layer_i, 0, 0, 0))],
        out_specs=pl.BlockSpec((tb, 2 * g * W), lambda i: (i, 0)),
        out_shape=jax.ShapeDtypeStruct((nb, 2 * g * W), F32),
        compiler_params=_cp(("parallel",)),
        name="cmp_prep",
    )(nsa_kv, cmp_pe, cmp_w)


def _top_n_mask(score, n_sel):
    nb = score.shape[-1]
    idx = lax.broadcasted_iota(jnp.int32, score.shape, 1)
    sel = jnp.zeros(score.shape, F32)
    work = score
    for _ in range(n_sel):
        mx = jnp.max(work, axis=-1, keepdims=True)
        first = jnp.min(jnp.where(work == mx, idx, nb), axis=-1, keepdims=True)
        hit = idx == first
        sel = jnp.where(hit, 1.0, sel)
        work = jnp.where(hit, -jnp.inf, work)
    return sel


def _nsa_prompt_kernel(qr_ref, qt_ref, ckv_ref, ks_ref, vs_ref, kw_ref, vw_ref, gate_ref, o_ref,
                       m_ref, l_ref, acc_ref, *, cfg, tq, rep, nb):
    gidx = pl.program_id(1)
    i = pl.program_id(2)
    W = LANES
    blk = cfg.cmp_block
    scale = W ** -0.5
    rows = rep * tq
    qpos = i * tq + lax.broadcasted_iota(jnp.int32, (tq, 1), 0)

    kc = ckv_ref[:, pl.ds(pl.multiple_of(gidx * W, W), W)]
    vc = ckv_ref[:, pl.ds(pl.multiple_of((cfg.b_kv + gidx) * W, W), W)]
    bidx = lax.broadcasted_iota(jnp.int32, (tq, nb), 1)
    cvalid = (bidx + 1) * blk <= qpos + 1
    imp = jnp.zeros((tq, nb), F32)
    o_cmp = []
    for r in range(rep):
        q = qr_ref[:, r * W:(r + 1) * W]
        sc = _dot_t(q.astype(BF16), kc.astype(BF16)) * scale
        sc = jnp.where(cvalid, sc, -jnp.inf)
        mx = jnp.max(sc, axis=-1, keepdims=True)
        mx = jnp.where(mx > -jnp.inf, mx, 0.0)
        e = jnp.exp(sc - mx)
        p = e / jnp.maximum(jnp.sum(e, axis=-1, keepdims=True), 1e-30)
        imp = imp + p
        o_cmp.append(jnp.dot(p.astype(BF16), vc.astype(BF16), preferred_element_type=F32))
    cur = qpos // blk
    forced = (bidx == 0) | ((bidx <= cur) & (bidx > cur - cfg.n_local))
    score = jnp.where(bidx > cur, -jnp.inf, jnp.where(forced, cfg.forced, imp))
    sel = _top_n_mask(score, min(cfg.top_n, nb)).astype(BF16)

    q4 = jnp.concatenate([qt_ref[:, r * W:(r + 1) * W] for r in range(rep)], axis=0)
    q4 = (q4 * scale).astype(BF16)
    row = lax.broadcasted_iota(jnp.int32, (rows, tq), 0) % tq
    colk = lax.broadcasted_iota(jnp.int32, (rows, tq), 1)
    bpt = tq // blk

    def reset():
        m_ref[...] = jnp.full(m_ref.shape, NEG, F32)
        l_ref[...] = jnp.zeros(l_ref.shape, F32)
        acc_ref[...] = jnp.zeros(acc_ref.shape, F32)

    def sel_tile(j, diag):
        k = ks_ref[pl.ds(j * tq, tq), :].astype(BF16)
        v = vs_ref[pl.ds(j * tq, tq), :].astype(BF16)
        er = lax.broadcasted_iota(jnp.int32, (nb, tq), 0)
        ec = lax.broadcasted_iota(jnp.int32, (nb, tq), 1)
        expand = jnp.where(er == j * bpt + ec // blk, 1.0, 0.0).astype(BF16)
        keep = jnp.dot(sel, expand, preferred_element_type=F32) > 0.5
        keep = jnp.concatenate([keep] * rep, axis=0)
        if diag:
            keep = keep & (colk <= row)
        s = jnp.where(keep, _dot_t(q4, k), NEG)
        _softmax_step(s, m_ref, l_ref, acc_ref, v)

    reset()

    def sel_body(j, c):
        sel_tile(j, False)
        return c

    lax.fori_loop(0, i, sel_body, 0)
    sel_tile(i, True)
    o_sel = acc_ref[...] / l_ref[...]

    def win_tile(j):
        k = kw_ref[pl.ds(j * tq, tq), :].astype(BF16)
        v = vw_ref[pl.ds(j * tq, tq), :].astype(BF16)
        kpos = j * tq + colk
        qp = i * tq + row
        keep = (kpos <= qp) & (kpos > qp - cfg.window)
        s = jnp.where(keep, _dot_t(q4, k), NEG)
        _softmax_step(s, m_ref, l_ref, acc_ref, v)

    reset()
    wt = -(-cfg.window // tq)

    def win_body(j, c):
        win_tile(j)
        return c

    lax.fori_loop(jnp.maximum(i - wt, 0), i + 1, win_body, 0)
    o_win = acc_ref[...] / l_ref[...]

    gates = jax.nn.sigmoid(gate_ref[...])
    for r in range(rep):
        h = gidx * rep + r
        lane = lax.broadcasted_iota(jnp.int32, gates.shape, 1)

        def gcol(c):
            return jnp.sum(jnp.where(lane == 3 * h + c, gates, 0.0), axis=-1, keepdims=True)

        o_ref[:, r * W:(r + 1) * W] = (gcol(0) * o_cmp[r] + gcol(1) * o_sel[r * tq:(r + 1) * tq]
                                       + gcol(2) * o_win[r * tq:(r + 1) * tq])


def nsa_prompt(cfg, bq_raw, bq_rot, cmp_kv, nsa_kv, win_kv, gates, bsz, n_t, tq=256):
    tq = _pick(n_t, tq)
    assert tq % cfg.cmp_block == 0
    nq = n_t // tq
    g, W = cfg.b_kv, LANES
    rep = cfg.b_heads // g
    nb = n_t // cfg.cmp_block
    qspec = pl.BlockSpec((tq, rep * W), lambda b, gi, i: (b * nq + i, gi))

    def kvspec(off):
        return pl.BlockSpec((n_t, W), lambda b, gi, i: (b, off + gi))

    return pl.pallas_call(
        functools.partial(_nsa_prompt_kernel, cfg=cfg, tq=tq, rep=rep, nb=nb),
        grid=(bsz, g, nq),
        in_specs=[qspec, qspec,
                  pl.BlockSpec((nb, 2 * g * W), lambda b, gi, i: (b, 0)),
                  kvspec(2 * g), kvspec(3 * g), kvspec(0), kvspec(g),
                  pl.BlockSpec((tq, W), lambda b, gi, i: (b * nq + i, 0))],
        out_specs=qspec,
        out_shape=jax.ShapeDtypeStruct((bsz * n_t, cfg.b_heads * W), F32),
        scratch_shapes=[pltpu.VMEM((rep * tq, 1), F32), pltpu.VMEM((rep * tq, 1), F32),
                        pltpu.VMEM((rep * tq, W), F32)],
        compiler_params=_cp(("parallel", "parallel", "arbitrary")),
        name="nsa_prompt",
    )(bq_raw, bq_rot, cmp_kv, nsa_kv, nsa_kv, win_kv, win_kv, gates)


def _softplus(z):
    return jnp.maximum(z, 0.0) + jnp.log1p(jnp.exp(-jnp.abs(z)))


def _sb_tile(q, k, v, upper, carry, mask):
    z = _dot_t(q, k)
    sp = _softplus(z)
    lm = -sp if mask is None else jnp.where(mask, -sp, 0.0)
    hi = lm.astype(BF16)
    lo = (lm - hi.astype(F32)).astype(BF16)
    between = (jnp.dot(hi, upper, preferred_element_type=F32)
               + jnp.dot(lo, upper, preferred_element_type=F32)) + carry
    w = jnp.exp(z - sp + between)
    if mask is not None:
        w = jnp.where(mask, w, 0.0)
    out = jnp.dot(w.astype(BF16), v, preferred_element_type=F32)
    return out, carry + jnp.sum(lm, axis=-1, keepdims=True)


def _sb_prompt_kernel(q_ref, k_ref, v_ref, o_ref, c_ref, acc_ref, *, tq):
    i = pl.program_id(2)
    q = (q_ref[...] * (LANES ** -0.5)).astype(BF16)
    r = lax.broadcasted_iota(jnp.int32, (tq, tq), 0)
    c = lax.broadcasted_iota(jnp.int32, (tq, tq), 1)
    upper = jnp.where(r > c, 1.0, 0.0).astype(BF16)
    k = k_ref[pl.ds(i * tq, tq), :].astype(BF16)
    v = v_ref[pl.ds(i * tq, tq), :].astype(BF16)
    out, carry = _sb_tile(q, k, v, upper, jnp.zeros((tq, 1), F32), c < r)
    acc_ref[...] = out
    c_ref[...] = carry

    def body(t, x):
        j = i - 1 - t
        kk = k_ref[pl.ds(j * tq, tq), :].astype(BF16)
        vv = v_ref[pl.ds(j * tq, tq), :].astype(BF16)
        o, cnew = _sb_tile(q, kk, vv, upper, c_ref[...], None)
        acc_ref[...] += o
        c_ref[...] = cnew
        return x

    lax.fori_loop(0, i, body, 0)
    o_ref[...] = acc_ref[...]


def sb_prompt(cfg, qkv, bsz, n_t, tq=256):
    tq = _pick(n_t, tq)
    nq = n_t // tq
    ch, W = cfg.c_heads, LANES
    return pl.pallas_call(
        functools.partial(_sb_prompt_kernel, tq=tq),
        grid=(bsz, ch, nq),
        in_specs=[pl.BlockSpec((tq, W), lambda b, h, i: (b * nq + i, h)),
                  pl.BlockSpec((n_t, W), lambda b, h, i: (b, ch + h)),
                  pl.BlockSpec((n_t, W), lambda b, h, i: (b, 2 * ch + h))],
        out_specs=pl.BlockSpec((tq, W), lambda b, h, i: (b * nq + i, h)),
        out_shape=jax.ShapeDtypeStruct((bsz * n_t, ch * W), F32),
        scratch_shapes=[pltpu.VMEM((tq, 1), F32), pltpu.VMEM((tq, W), F32)],
        compiler_params=_cp(("parallel", "parallel", "arbitrary")),
        name="sb_prompt",
    )(qkv, qkv, qkv)


def _router_kernel(h_ref, w_ref, o_ref, *, n_exp, top_k):
    logits = jnp.dot(h_ref[...], w_ref[...], preferred_element_type=F32, precision=lax.Precision.HIGHEST)
    idx = lax.broadcasted_iota(jnp.int32, logits.shape, 1)
    valid = idx < n_exp
    work = jnp.where(valid, logits, -jnp.inf)
    picked = jnp.zeros(logits.shape, jnp.bool_)
    for _ in range(top_k):
        mx = jnp.max(work, axis=-1, keepdims=True)
        first = jnp.min(jnp.where(work == mx, idx, logits.shape[-1]), axis=-1, keepdims=True)
        hit = idx == first
        picked = picked | hit
        work = jnp.where(hit, -jnp.inf, work)
    top = jnp.max(jnp.where(valid, logits, -jnp.inf), axis=-1, keepdims=True)
    e = jnp.where(picked, jnp.exp(logits - top), 0.0)
    o_ref[...] = e / jnp.sum(e, axis=-1, keepdims=True)


def router(cfg, h, w_router_pad, tm=512):
    m, d = h.shape
    tm = _pick(m, tm)
    return pl.pallas_call(
        functools.partial(_router_kernel, n_exp=cfg.n_exp, top_k=cfg.top_k),
        grid=(m // tm,),
        in_specs=[pl.BlockSpec((tm, d), lambda i: (i, 0)),
                  pl.BlockSpec((d, LANES), lambda i: (0, 0))],
        out_specs=pl.BlockSpec((tm, LANES), lambda i: (i, 0)),
        out_shape=jax.ShapeDtypeStruct((m, LANES), F32),
        compiler_params=_cp(("parallel",)),
        name="router",
    )(h, w_router_pad)


def _moe_down_kernel(h_ref, w_ref, comb_ref, prev_ref, o_ref, acc_ref, *, nk, e):
    k = pl.program_id(2)
    part = jnp.dot(h_ref[...], w_ref[...].astype(BF16), preferred_element_type=F32)

    @pl.when(k == 0)
    def _():
        acc_ref[...] = part

    @pl.when(k > 0)
    def _():
        acc_ref[...] += part

    @pl.when(k == nk - 1)
    def _():
        comb = comb_ref[...]
        lane = lax.broadcasted_iota(jnp.int32, comb.shape, 1)
        ce = jnp.sum(jnp.where(lane == e, comb, 0.0), axis=-1, keepdims=True)
        o_ref[...] = prev_ref[...] + ce * acc_ref[...]


def moe_down_acc(h, w_down, lead, comb, prev, e, tm=1024, tn=512, tk=512):
    m, kdim = h.shape
    n = w_down.shape[-1]
    tm, tn, tk = _pick(m, tm), _pick(n, tn), _pick(kdim, tk)
    nk = kdim // tk
    return pl.pallas_call(
        functools.partial(_moe_down_kernel, nk=nk, e=e),
        grid=(m // tm, n // tn, nk),
        in_specs=[pl.BlockSpec((tm, tk), lambda i, j, k: (i, k)),
                  pl.BlockSpec((None, None, tk, tn), lambda i, j, k: tuple(lead) + (k, j)),
                  pl.BlockSpec((tm, LANES), lambda i, j, k: (i, 0)),
                  pl.BlockSpec((tm, tn), lambda i, j, k: (i, j))],
        out_specs=pl.BlockSpec((tm, tn), lambda i, j, k: (i, j)),
        out_shape=jax.ShapeDtypeStruct((m, n), F32),
        scratch_shapes=[pltpu.VMEM((tm, tn), F32)],
        input_output_aliases={3: 0},
        compiler_params=_cp(("parallel", "parallel", "arbitrary")),
        name="moe_down_acc",
    )(h, w_down, comb, prev)


def moe_dense(cfg, h, w_router_pad, moe_w_gu, moe_w_down, layer_i):
    comb = router(cfg, h, w_router_pad)
    out = jnp.zeros(h.shape, F32)
    for e in range(cfg.n_exp):
        a = swiglu_up(h, moe_w_gu, (layer_i, e), cfg.d_ff)
        out = moe_down_acc(a, moe_w_down, (layer_i, e), comb, out, e)
    return out


def _gather_pages(pool, page_table):
    rows = pool[page_table]
    return rows.reshape((page_table.shape[0], page_table.shape[1] * pool.shape[1]) + pool.shape[2:])


def _diff_sample_jnp(cfg, aq, all_k, all_v, lam, subln_g, lam_init):
    b = aq.shape[0]
    q = aq.reshape(b, cfg.a_heads, 2, cfg.a_qk)
    k = all_k.reshape(b, -1, cfg.a_heads, 2, cfg.a_qk)
    sc = jnp.einsum('bhmd,bkhmd->bhmk', q, k, precision=lax.Precision.HIGHEST) * cfg.a_qk ** -0.5
    p = jax.nn.softmax(sc, axis=-1)
    w = p[:, :, 0] - lam * p[:, :, 1]
    o = jnp.einsum('bhk,bkhd->bhd', w, all_v, precision=lax.Precision.HIGHEST)
    o = o * lax.rsqrt(jnp.mean(o * o, axis=-1, keepdims=True) + cfg.rms_eps) * subln_g * (1.0 - lam_init)
    return o.reshape(b, -1)


def _forward(cfg, x_prompt, x_sample, cache_diff_kv, cache_nsa_kv, state_nsa_win, cache_sb_kv, page_table,
             even_w_in, even_w_out, diff_lambda, diff_subln_g, nsa_cmp_pe, nsa_cmp_w,
             odd_w_in, odd_w_out, ln1_g, ln1_b, ln2_g, ln2_b,
             ffn_w_gu, ffn_w_down, moe_router, moe_w_gu, moe_w_down):
    bsz, n_t, d = x_prompt.shape
    dbsz = x_sample.shape[0]
    past_len = page_table.shape[1] * cfg.page
    W = LANES
    xp = x_prompt.reshape(bsz * n_t, d)
    xs = x_sample.reshape(dbsz, d)
    tabs_p = rope_tables(cfg, jnp.arange(n_t, dtype=jnp.int32))
    tabs_s = rope_tables(cfg, jnp.full((dbsz,), past_len, dtype=jnp.int32))
    outs = {k: [] for k in ("diff_p", "diff_s", "nsa_p", "nsa_s", "win_p", "win_s", "sb_p", "sb_s")}
    n_gate = 3 * cfg.b_heads
    for layer in range(cfg.depth):
        i = layer // 2
        if layer % 2 == 0:
            lam_init = 0.8 - 0.6 * math.exp(-0.3 * layer)
            w_gate = jnp.pad(even_w_in[i][:, cfg.even_main:], ((0, 0), (0, W - n_gate)))
            proj = matmul(xp, even_w_in, (i,), 0, cfg.even_main)
            gates = matmul(xp, w_gate)
            aq, dkv, bqr, bqt, nkv, wkv = even_split(cfg, proj, tabs_p, n_t)
            a_o = diff_prompt(cfg, aq, dkv, diff_lambda, diff_subln_g, i, lam_init, bsz, n_t)
            ckv = cmp_prep(cfg, nkv, nsa_cmp_pe, nsa_cmp_w, i)
            b_o = nsa_prompt(cfg, bqr, bqt, ckv, nkv, wkv, gates, bsz, n_t)
            mp = matmul(jnp.concatenate([a_o, b_o], axis=-1), even_w_out, (i,))
            keep = min(cfg.window, n_t)
            outs["diff_p"].append(dkv.reshape(bsz, n_t, 2, cfg.a_heads, W))
            outs["nsa_p"].append(nkv.reshape(bsz, n_t, 4, cfg.b_kv, W))
            outs["win_p"].append(wkv.reshape(bsz, n_t, 2, cfg.b_kv, W)[:, n_t - keep:])
            ms, dks, nks, wsn = _even_sample(cfg, xs, even_w_in, w_gate, even_w_out, i, tabs_s, lam_init,
                                             cache_diff_kv, cache_nsa_kv, state_nsa_win, page_table,
                                             diff_lambda, diff_subln_g, nsa_cmp_pe, nsa_cmp_w, past_len)
            outs["diff_s"].append(dks)
            outs["nsa_s"].append(nks)
            outs["win_s"].append(wsn)
        else:
            qkv = matmul(xp, odd_w_in, (i,))
            o = sb_prompt(cfg, qkv, bsz, n_t)
            mp = matmul(o, odd_w_out, (i,))
            outs["sb_p"].append(qkv[:, cfg.c_heads * W:].reshape(bsz, n_t, 2, cfg.c_heads, W))
            ms, sks = _odd_sample(cfg, xs, odd_w_in, odd_w_out, i, cache_sb_kv, page_table)
            outs["sb_s"].append(sks)
        xp = add_ln(cfg, xp, mp, ln1_g, ln1_b, layer)
        xs = add_ln(cfg, xs, ms, ln1_g, ln1_b, layer)
        if layer % 2 == 0:
            fp = matmul(swiglu_up(xp, ffn_w_gu, (i,), cfg.d_ff), ffn_w_down, (i,), tk=512)
            fs = matmul(swiglu_up(xs, ffn_w_gu, (i,), cfg.d_ff), ffn_w_down, (i,), tk=512)
        else:
            w_r = jnp.pad(moe_router[i], ((0, 0), (0, W - cfg.n_exp)))
            fp = moe_dense(cfg, xp, w_r, moe_w_gu, moe_w_down, i)
            fs = moe_dense(cfg, xs, w_r, moe_w_gu, moe_w_down, i)
        xp = add_ln(cfg, xp, fp, ln2_g, ln2_b, layer)
        xs = add_ln(cfg, xs, fs, ln2_g, ln2_b, layer)
    st = {k: jnp.stack(v) for k, v in outs.items()}
    return (xp.reshape(bsz, n_t, d), xs.reshape(dbsz, 1, d), st["diff_p"], st["diff_s"], st["nsa_p"], st["nsa_s"],
            st["win_p"], st["win_s"], st["sb_p"], st["sb_s"])


def _even_sample(cfg, xs, even_w_in, w_gate, even_w_out, i, tabs_s, lam_init,
                 cache_diff_kv, cache_nsa_kv, state_nsa_win, page_table,
                 diff_lambda, diff_subln_g, nsa_cmp_pe, nsa_cmp_w, past_len):
    dbsz = xs.shape[0]
    W, g = LANES, cfg.b_kv
    rep = cfg.b_heads // g
    hp = lax.Precision.HIGHEST
    proj = matmul(xs, even_w_in, (i,), 0, cfg.even_main)
    gates = jax.nn.sigmoid(matmul(xs, w_gate)[:, :3 * cfg.b_heads]).reshape(dbsz, cfg.b_heads, 3)
    aq, dkv, bqr, bqt, nkv, wkv = even_split(cfg, proj, tabs_s, dbsz)
    new_diff = dkv.reshape(dbsz, 1, 2, cfg.a_heads, W)
    new_nsa = nkv.reshape(dbsz, 1, 4, g, W)
    new_win = wkv.reshape(dbsz, 1, 2, g, W)
    all_diff = jnp.concatenate([_gather_pages(cache_diff_kv[i], page_table), new_diff], axis=1)
    all_nsa = jnp.concatenate([_gather_pages(cache_nsa_kv[i], page_table), new_nsa], axis=1)
    win = jnp.concatenate([state_nsa_win[i], new_win], axis=1)
    lv = diff_lambda[i]
    lam = jnp.exp(jnp.sum(lv[0] * lv[1])) - jnp.exp(jnp.sum(lv[2] * lv[3])) + lam_init
    a_o = _diff_sample_jnp(cfg, aq, all_diff[:, :, 0], all_diff[:, :, 1], lam, diff_subln_g[i], lam_init)
    blk = cfg.cmp_block
    seq_len = past_len + 1
    nb_c = seq_len // blk
    blocks = all_nsa[:, :nb_c * blk, 0:2].reshape(dbsz, nb_c, blk, 2, g, W)
    summ = jnp.mean(blocks + jnp.transpose(nsa_cmp_pe[i], (1, 0, 2))[:, :, None, :], axis=2)
    summ = jnp.einsum('bnjgd,jde->bnjge', summ, nsa_cmp_w[i], precision=hp)
    k_cmp, v_cmp = summ[:, :, 0], summ[:, :, 1]
    scale = W ** -0.5
    qg = bqr.reshape(dbsz, g, rep, W)
    sc = jnp.einsum('bgrd,bngd->bgrn', qg, k_cmp, precision=hp) * scale
    p_cmp = jax.nn.softmax(sc, axis=-1)
    o_cmp = jnp.einsum('bgrn,bngd->bgrd', p_cmp, v_cmp, precision=hp).reshape(dbsz, cfg.b_heads, W)
    nb_s = -(-seq_len // blk)
    imp = jnp.pad(jnp.sum(p_cmp, axis=2), ((0, 0), (0, 0), (0, nb_s - nb_c)))
    bidx = jnp.arange(nb_s, dtype=jnp.int32)
    cur = past_len // blk
    forced = (bidx == 0) | ((bidx <= cur) & (bidx > cur - cfg.n_local))
    score = jnp.where(bidx > cur, -jnp.inf, jnp.where(forced, cfg.forced, imp))
    _, sel_idx = lax.top_k(score, min(cfg.top_n, nb_s))
    sel = jnp.pad(all_nsa[:, :, 2:4], ((0, 0), (0, nb_s * blk - seq_len), (0, 0), (0, 0), (0, 0)))
    sel_blocks = jnp.transpose(sel.reshape(dbsz, nb_s, blk, 2, g, W), (0, 4, 1, 2, 3, 5))
    gath = sel_blocks[jnp.arange(dbsz)[:, None, None], jnp.arange(g)[None, :, None], sel_idx]
    gath = gath.reshape(dbsz, g, -1, 2, W)
    kpos = (sel_idx[..., None] * blk + jnp.arange(blk, dtype=jnp.int32)).reshape(dbsz, g, -1)
    qt = bqt.reshape(dbsz, g, rep, W)
    ssc = jnp.einsum('bgrd,bgkd->bgrk', qt, gath[..., 0, :], precision=hp) * scale
    ps = jax.nn.softmax(jnp.where((kpos <= past_len)[:, :, None], ssc, -jnp.inf), axis=-1)
    o_sel = jnp.einsum('bgrk,bgkd->bgrd', ps, gath[..., 1, :], precision=hp).reshape(dbsz, cfg.b_heads, W)
    keep = state_nsa_win.shape[2]
    kp = past_len - keep + jnp.arange(keep + 1, dtype=jnp.int32)
    wmask = kp > past_len - cfg.window
    wsc = jnp.einsum('bgrd,bkgd->bgrk', qt, win[:, :, 0], precision=hp) * scale
    pw = jax.nn.softmax(jnp.where(wmask, wsc, -jnp.inf), axis=-1)
    o_win = jnp.einsum('bgrk,bkgd->bgrd', pw, win[:, :, 1], precision=hp).reshape(dbsz, cfg.b_heads, W)
    b_o = gates[..., 0:1] * o_cmp + gates[..., 1:2] * o_sel + gates[..., 2:3] * o_win
    mixed = jnp.concatenate([a_o, b_o.reshape(dbsz, -1)], axis=-1)
    ms = matmul(mixed, even_w_out, (i,))
    return ms, new_diff, new_nsa, win[:, win.shape[1] - keep:]


def _odd_sample(cfg, xs, odd_w_in, odd_w_out, i, cache_sb_kv, page_table):
    dbsz = xs.shape[0]
    ch, W = cfg.c_heads, LANES
    hp = lax.Precision.HIGHEST
    qkv = matmul(xs, odd_w_in, (i,))
    q = qkv[:, :ch * W].reshape(dbsz, ch, W)
    new_kv = qkv[:, ch * W:].reshape(dbsz, 1, 2, ch, W)
    past = _gather_pages(cache_sb_kv[i], page_table)
    z = jnp.einsum('bhd,bkhd->bhk', q, past[:, :, 0], precision=hp) * W ** -0.5
    log1m = -jax.nn.softplus(z)
    between = lax.cumsum(log1m, axis=2, reverse=True) - log1m
    w = jnp.exp(jax.nn.log_sigmoid(z) + between)
    o = jnp.einsum('bhk,bkhd->bhd', w, past[:, :, 1], precision=hp).reshape(dbsz, ch * W)
    return matmul(o, odd_w_out, (i,)), new_kv


def kernel(x_prompt, x_sample, cache_diff_kv, cache_nsa_kv, state_nsa_win, cache_sb_kv, page_table, even_w_in, even_w_out, diff_lambda, diff_subln_g, nsa_cmp_pe, nsa_cmp_w, odd_w_in, odd_w_out, ln1_g, ln1_b, ln2_g, ln2_b, ffn_w_gu, ffn_w_down, moe_router, moe_w_gu, moe_w_down):
    return _forward(Cfg(), x_prompt, x_sample, cache_diff_kv, cache_nsa_kv, state_nsa_win, cache_sb_kv, page_table,
                    even_w_in, even_w_out, diff_lambda, diff_subln_g, nsa_cmp_pe, nsa_cmp_w,
                    odd_w_in, odd_w_out, ln1_g, ln1_b, ln2_g, ln2_b,
                    ffn_w_gu, ffn_w_down, moe_router, moe_w_gu, moe_w_down)
```

```python
import functools
import math
from typing import NamedTuple

import numpy as np
import jax
import jax.numpy as jnp
from jax import lax
from jax.experimental import pallas as pl
from jax.experimental.pallas import tpu as pltpu

F32 = jnp.float32
BF16 = jnp.bfloat16

LANES = 128
VMEM_LIMIT = 56 * 1024 * 1024
NEG = -1.0e30
SB_EXIT = -104.0


class Cfg(NamedTuple):
    d_model: int = 2048
    depth: int = 4
    page: int = 128
    a_heads: int = 8
    a_qk: int = 64
    b_heads: int = 8
    b_kv: int = 2
    cmp_block: int = 64
    top_n: int = 16
    n_local: int = 2
    window: int = 512
    forced: float = 1.0e4
    c_heads: int = 16
    d_ff: int = 5632
    n_exp: int = 8
    top_k: int = 2
    theta: float = 10000.0
    q_block: int = 128
    ln_eps: float = 1e-5
    rms_eps: float = 1e-5

    @property
    def alpha(self):
        return (2 * self.depth) ** 0.25

    @property
    def even_main(self):
        return 4 * self.a_heads * LANES + 6 * self.b_kv * LANES


def _cp(sem, vmem=VMEM_LIMIT):
    return pltpu.CompilerParams(dimension_semantics=sem, vmem_limit_bytes=vmem)


def _pick(n, pref):
    if n <= pref:
        return n
    t = pref
    while n % t:
        t //= 2
    return t


def _mm_kernel(x_ref, w_ref, o_ref, acc_ref, *, nk):
    part = jnp.dot(x_ref[...].astype(BF16), w_ref[...].astype(BF16), preferred_element_type=F32)
    if nk == 1:
        o_ref[...] = part.astype(o_ref.dtype)
        return
    k = pl.program_id(2)

    @pl.when(k == 0)
    def _():
        acc_ref[...] = part

    @pl.when(k > 0)
    def _():
        acc_ref[...] += part

    @pl.when(k == nk - 1)
    def _():
        o_ref[...] = acc_ref[...].astype(o_ref.dtype)


def _w_spec(w, lead, tk, tn, cb0):
    nlead = len(lead)
    return pl.BlockSpec((None,) * nlead + (tk, tn), lambda i, j, k: tuple(lead) + (k, cb0 + j))


def matmul(x, w, lead=(), col0=0, n=None, tm=1024, tn=512, tk=None, out_dtype=F32):
    m, kdim = x.shape
    n = w.shape[-1] - col0 if n is None else n
    tm = _pick(m, tm)
    tn = _pick(n, tn)
    tk = kdim if tk is None else _pick(kdim, tk)
    assert col0 % tn == 0 and m % tm == 0 and n % tn == 0 and kdim % tk == 0
    nk = kdim // tk
    return pl.pallas_call(
        functools.partial(_mm_kernel, nk=nk),
        grid=(m // tm, n // tn, nk),
        in_specs=[pl.BlockSpec((tm, tk), lambda i, j, k: (i, k)),
                  _w_spec(w, lead, tk, tn, col0 // tn)],
        out_specs=pl.BlockSpec((tm, tn), lambda i, j, k: (i, j)),
        out_shape=jax.ShapeDtypeStruct((m, n), out_dtype),
        scratch_shapes=[pltpu.VMEM((tm, tn), F32)],
        compiler_params=_cp(("parallel", "parallel", "arbitrary")),
        name="matmul",
    )(x, w)


def _swiglu_kernel(x_ref, wg_ref, wu_ref, o_ref):
    x = x_ref[...].astype(BF16)
    g = jnp.dot(x, wg_ref[...].astype(BF16), preferred_element_type=F32)
    u = jnp.dot(x, wu_ref[...].astype(BF16), preferred_element_type=F32)
    o_ref[...] = (g * jax.nn.sigmoid(g) * u).astype(o_ref.dtype)


def swiglu_up(x, w_gu, lead, d_ff, tm=1024, tn=512):
    m, kdim = x.shape
    tm = _pick(m, tm)
    tn = _pick(d_ff, tn)
    nb = d_ff // tn
    nlead = len(lead)
    wblk = (None,) * nlead + (kdim, tn)
    return pl.pallas_call(
        _swiglu_kernel,
        grid=(m // tm, nb),
        in_specs=[pl.BlockSpec((tm, kdim), lambda i, j: (i, 0)),
                  pl.BlockSpec(wblk, lambda i, j: tuple(lead) + (0, j)),
                  pl.BlockSpec(wblk, lambda i, j: tuple(lead) + (0, nb + j))],
        out_specs=pl.BlockSpec((tm, tn), lambda i, j: (i, j)),
        out_shape=jax.ShapeDtypeStruct((m, d_ff), BF16),
        compiler_params=_cp(("parallel", "parallel")),
        name="swiglu_up",
    )(x, w_gu, w_gu)


def _ln_kernel(x_ref, f_ref, g_ref, b_ref, o_ref, *, alpha, eps):
    y = alpha * x_ref[...] + f_ref[...].astype(F32)
    mu = jnp.mean(y, axis=-1, keepdims=True)
    d = y - mu
    var = jnp.mean(d * d, axis=-1, keepdims=True)
    o_ref[...] = d * lax.rsqrt(var + eps) * g_ref[...] + b_ref[...]


def add_ln(cfg, x, f, g, b, layer, tm=256):
    m, d = x.shape
    tm = _pick(m, tm)
    return pl.pallas_call(
        functools.partial(_ln_kernel, alpha=cfg.alpha, eps=cfg.ln_eps),
        grid=(m // tm,),
        in_specs=[pl.BlockSpec((tm, d), lambda i: (i, 0)),
                  pl.BlockSpec((tm, d), lambda i: (i, 0)),
                  pl.BlockSpec((None, 1, d), lambda i: (layer, 0, 0)),
                  pl.BlockSpec((None, 1, d), lambda i: (layer, 0, 0))],
        out_specs=pl.BlockSpec((tm, d), lambda i: (i, 0)),
        out_shape=jax.ShapeDtypeStruct((m, d), F32),
        compiler_params=_cp(("parallel",)),
        name="add_ln",
    )(x, f, g.reshape(-1, 1, d), b.reshape(-1, 1, d))


def rope_tables(cfg, pos):
    out = []
    for d in (cfg.a_qk, LANES):
        half = d // 2
        inv = cfg.theta ** (-jnp.arange(half, dtype=F32) / half)
        ang = pos.astype(F32)[:, None] * inv[None, :]
        c, s = jnp.cos(ang), jnp.sin(ang)
        reps = LANES // d
        out.append(jnp.tile(jnp.concatenate([c, c], axis=-1), (1, reps)))
        out.append(jnp.tile(jnp.concatenate([-s, s], axis=-1), (1, reps)))
    return out


def _rope(x, c, s, d):
    if d == LANES:
        partner = pltpu.roll(x, LANES // 2, 1)
    else:
        lane = lax.broadcasted_iota(jnp.int32, x.shape, 1)
        lo = (lane % d) < (d // 2)
        partner = jnp.where(lo, pltpu.roll(x, LANES - d // 2, 1), pltpu.roll(x, d // 2, 1))
    return x * c + partner * s


def _split_kernel(p_ref, c64_ref, s64_ref, c128_ref, s128_ref,
                  aq_ref, dkv_ref, bqr_ref, bqt_ref, nkv_ref, win_ref, *, cfg):
    c64, s64, c128, s128 = c64_ref[...], s64_ref[...], c128_ref[...], s128_ref[...]
    ah, g = cfg.a_heads, cfg.b_kv
    W = LANES

    def col(i):
        return p_ref[:, i * W:(i + 1) * W]

    for h in range(ah):
        aq_ref[:, h * W:(h + 1) * W] = _rope(col(h), c64, s64, cfg.a_qk)
        dkv_ref[:, h * W:(h + 1) * W] = _rope(col(ah + h), c64, s64, cfg.a_qk)
        dkv_ref[:, (ah + h) * W:(ah + h + 1) * W] = col(2 * ah + h)
    for h in range(cfg.b_heads):
        x = col(3 * ah + h)
        bqr_ref[:, h * W:(h + 1) * W] = x
        bqt_ref[:, h * W:(h + 1) * W] = _rope(x, c128, s128, W)
    base = 3 * ah + cfg.b_heads
    for s in range(4):
        for j in range(g):
            x = col(base + s * g + j)
            nkv_ref[:, (s * g + j) * W:(s * g + j + 1) * W] = _rope(x, c128, s128, W) if s == 2 else x
    base += 4 * g
    for s in range(2):
        for j in range(g):
            x = col(base + s * g + j)
            win_ref[:, (s * g + j) * W:(s * g + j + 1) * W] = _rope(x, c128, s128, W) if s == 0 else x


def even_split(cfg, proj, tabs, n_pos):
    m = proj.shape[0]
    tm = _pick(n_pos, 256)
    npb = n_pos // tm
    ah, g, W = cfg.a_heads, cfg.b_kv, LANES
    widths = (ah * W, 2 * ah * W, cfg.b_heads * W, cfg.b_heads * W, 4 * g * W, 2 * g * W)
    tab_spec = pl.BlockSpec((tm, W), lambda i: (i % npb, 0))
    return pl.pallas_call(
        functools.partial(_split_kernel, cfg=cfg),
        grid=(m // tm,),
        in_specs=[pl.BlockSpec((tm, proj.shape[1]), lambda i: (i, 0))] + [tab_spec] * 4,
        out_specs=[pl.BlockSpec((tm, w), lambda i: (i, 0)) for w in widths],
        out_shape=[jax.ShapeDtypeStruct((m, w), F32) for w in widths],
        compiler_params=_cp(("parallel",)),
        name="even_split",
    )(proj, *tabs)


def _softmax_step(s, m_ref, l_ref, acc_ref, v):
    m_old = m_ref[...]
    m_new = jnp.maximum(m_old, jnp.max(s, axis=-1, keepdims=True))
    a = jnp.exp(m_old - m_new)
    p = jnp.exp(s - m_new)
    l_ref[...] = a * l_ref[...] + jnp.sum(p, axis=-1, keepdims=True)
    acc_ref[...] = a * acc_ref[...] + jnp.dot(p.astype(BF16), v, preferred_element_type=F32)
    m_ref[...] = m_new


def _dot_t(a, b):
    return lax.dot_general(a, b, (((1,), (1,)), ((), ())), preferred_element_type=F32)


def _diff_lambda(lam_ref, lam_init):
    lv = lam_ref[...]
    a = jnp.sum(lv[0:1] * lv[1:2], axis=-1, keepdims=True)
    b = jnp.sum(lv[2:3] * lv[3:4], axis=-1, keepdims=True)
    return jnp.exp(a) - jnp.exp(b) + lam_init


def _diff_finish(o1, o2, lam, g, lam_init, eps):
    o = o1 - lam * o2
    o = o * lax.rsqrt(jnp.mean(o * o, axis=-1, keepdims=True) + eps)
    return o * g * (1.0 - lam_init)


def _diff_prompt_kernel(q_ref, k_ref, v_ref, lam_ref, g_ref, o_ref, m_ref, l_ref, acc_ref,
                        *, tq, d_qk, lam_init, eps):
    i = pl.program_id(2)
    q = q_ref[...] * (d_qk ** -0.5)
    lane = lax.broadcasted_iota(jnp.int32, q.shape, 1)
    q2 = jnp.concatenate([jnp.where(lane < d_qk, q, 0.0), jnp.where(lane >= d_qk, q, 0.0)], axis=0).astype(BF16)
    m_ref[...] = jnp.full(m_ref.shape, NEG, F32)
    l_ref[...] = jnp.zeros(l_ref.shape, F32)
    acc_ref[...] = jnp.zeros(acc_ref.shape, F32)

    def tile(j, masked):
        k = k_ref[pl.ds(j * tq, tq), :].astype(BF16)
        v = v_ref[pl.ds(j * tq, tq), :].astype(BF16)
        s = _dot_t(q2, k)
        if masked:
            row = lax.broadcasted_iota(jnp.int32, s.shape, 0) % tq
            colk = lax.broadcasted_iota(jnp.int32, s.shape, 1)
            s = jnp.where(colk <= row, s, NEG)
        _softmax_step(s, m_ref, l_ref, acc_ref, v)

    def body(j, c):
        tile(j, False)
        return c

    lax.fori_loop(0, i, body, 0)
    tile(i, True)
    o = acc_ref[...] / l_ref[...]
    lam = _diff_lambda(lam_ref, lam_init)
    o_ref[...] = _diff_finish(o[:tq], o[tq:], lam, g_ref[...], lam_init, eps)


def diff_prompt(cfg, aq, dkv, lam_vec, subln_g, layer_i, lam_init, bsz, n_t, tq=256):
    tq = _pick(n_t, tq)
    nq = n_t // tq
    ah, W = cfg.a_heads, LANES
    return pl.pallas_call(
        functools.partial(_diff_prompt_kernel, tq=tq, d_qk=cfg.a_qk, lam_init=lam_init, eps=cfg.rms_eps),
        grid=(bsz, ah, nq),
        in_specs=[pl.BlockSpec((tq, W), lambda b, h, i: (b * nq + i, h)),
                  pl.BlockSpec((n_t, W), lambda b, h, i: (b, h)),
                  pl.BlockSpec((n_t, W), lambda b, h, i: (b, ah + h)),
                  pl.BlockSpec((None, 4, cfg.a_qk), lambda b, h, i: (layer_i, 0, 0)),
                  pl.BlockSpec((None, 1, W), lambda b, h, i: (layer_i, 0, 0))],
        out_specs=pl.BlockSpec((tq, W), lambda b, h, i: (b * nq + i, h)),
        out_shape=jax.ShapeDtypeStruct((bsz * n_t, ah * W), F32),
        scratch_shapes=[pltpu.VMEM((2 * tq, 1), F32), pltpu.VMEM((2 * tq, 1), F32), pltpu.VMEM((2 * tq, W), F32)],
        compiler_params=_cp(("parallel", "parallel", "arbitrary")),
        name="diff_prompt",
    )(aq, dkv, dkv, lam_vec, subln_g.reshape(-1, 1, W))


def _cmp_prep_kernel(kv_ref, pe_ref, w_ref, o_ref, *, blk, g):
    rows = kv_ref[...]
    tb = rows.shape[0] // blk
    r = lax.broadcasted_iota(jnp.int32, (tb, rows.shape[0]), 0)
    c = lax.broadcasted_iota(jnp.int32, (tb, rows.shape[0]), 1)
    pool = jnp.where(c // blk == r, 1.0 / blk, 0.0).astype(F32)
    mean = jnp.dot(pool, rows, preferred_element_type=F32, precision=lax.Precision.HIGHEST)
    for j in range(2):
        pe = jnp.mean(pe_ref[j], axis=0, keepdims=True)
        for gg in range(g):
            sl = slice((j * g + gg) * LANES, (j * g + gg + 1) * LANES)
            o_ref[:, sl] = jnp.dot((mean[:, sl] + pe).astype(BF16), w_ref[j].astype(BF16),
                                   preferred_element_type=F32)


def cmp_prep(cfg, nsa_kv, cmp_pe, cmp_w, layer_i):
    m = nsa_kv.shape[0]
    blk, g, W = cfg.cmp_block, cfg.b_kv, LANES
    nb = m // blk
    tb = _pick(nb, 8)
    return pl.pallas_call(
        functools.partial(_cmp_prep_kernel, blk=blk, g=g),
        grid=(nb // tb,),
        in_specs=[pl.BlockSpec((tb * blk, 2 * g * W), lambda i: (i, 0)),
                  pl.BlockSpec((None, 2, blk, W), lambda i: (layer_i, 0, 0, 0)),
                  pl.BlockSpec((None, 2, W, W), lambda i: (layer_i, 0, 0, 0))],
        out_specs=pl.BlockSpec((tb, 2 * g * W), lambda i: (i, 0)),
        out_shape=jax.ShapeDtypeStruct((nb, 2 * g * W), F32),
        compiler_params=_cp(("parallel",)),
        name="cmp_prep",
    )(nsa_kv, cmp_pe, cmp_w)


def _top_n_mask(score, n_sel):
    nb = score.shape[-1]
    idx = lax.broadcasted_iota(jnp.int32, score.shape, 1)
    sel = jnp.zeros(score.shape, F32)
    work = score
    for _ in range(n_sel):
        mx = jnp.max(work, axis=-1, keepdims=True)
        first = jnp.min(jnp.where(work == mx, idx, nb), axis=-1, keepdims=True)
        hit = idx == first
        sel = jnp.where(hit, 1.0, sel)
        work = jnp.where(hit, -jnp.inf, work)
    return sel


def _nsa_prompt_kernel(qr_ref, qt_ref, ckv_ref, ks_ref, vs_ref, kw_ref, vw_ref, gate_ref, o_ref,
                       m_ref, l_ref, acc_ref, *, cfg, tq, rep, nb):
    gidx = pl.program_id(1)
    i = pl.program_id(2)
    W = LANES
    blk = cfg.cmp_block
    scale = W ** -0.5
    rows = rep * tq
    qpos = i * tq + lax.broadcasted_iota(jnp.int32, (tq, 1), 0)

    kc = ckv_ref[:, pl.ds(pl.multiple_of(gidx * W, W), W)]
    vc = ckv_ref[:, pl.ds(pl.multiple_of((cfg.b_kv + gidx) * W, W), W)]
    bidx = lax.broadcasted_iota(jnp.int32, (tq, nb), 1)
    cvalid = (bidx + 1) * blk <= qpos + 1
    imp = jnp.zeros((tq, nb), F32)
    o_cmp = []
    for r in range(rep):
        q = qr_ref[:, r * W:(r + 1) * W]
        sc = _dot_t(q.astype(BF16), kc.astype(BF16)) * scale
        sc = jnp.where(cvalid, sc, -jnp.inf)
        mx = jnp.max(sc, axis=-1, keepdims=True)
        mx = jnp.where(mx > -jnp.inf, mx, 0.0)
        e = jnp.exp(sc - mx)
        p = e / jnp.maximum(jnp.sum(e, axis=-1, keepdims=True), 1e-30)
        imp = imp + p
        o_cmp.append(jnp.dot(p.astype(BF16), vc.astype(BF16), preferred_element_type=F32))
    cur = qpos // blk
    forced = (bidx == 0) | ((bidx <= cur) & (bidx > cur - cfg.n_local))
    score = jnp.where(bidx > cur, -jnp.inf, jnp.where(forced, cfg.forced, imp))
    sel = _top_n_mask(score, min(cfg.top_n, nb)).astype(BF16)

    q4 = jnp.concatenate([qt_ref[:, r * W:(r + 1) * W] for r in range(rep)], axis=0)
    q4 = (q4 * scale).astype(BF16)
    row = lax.broadcasted_iota(jnp.int32, (rows, tq), 0) % tq
    colk = lax.broadcasted_iota(jnp.int32, (rows, tq), 1)
    bpt = tq // blk

    def reset():
        m_ref[...] = jnp.full(m_ref.shape, NEG, F32)
        l_ref[...] = jnp.zeros(l_ref.shape, F32)
        acc_ref[...] = jnp.zeros(acc_ref.shape, F32)

    def sel_tile(j, diag):
        k = ks_ref[pl.ds(j * tq, tq), :].astype(BF16)
        v = vs_ref[pl.ds(j * tq, tq), :].astype(BF16)
        er = lax.broadcasted_iota(jnp.int32, (nb, tq), 0)
        ec = lax.broadcasted_iota(jnp.int32, (nb, tq), 1)
        expand = jnp.where(er == j * bpt + ec // blk, 1.0, 0.0).astype(BF16)
        keep = jnp.dot(sel, expand, preferred_element_type=F32) > 0.5
        keep = jnp.concatenate([keep] * rep, axis=0)
        if diag:
            keep = keep & (colk <= row)
        s = jnp.where(keep, _dot_t(q4, k), NEG)
        _softmax_step(s, m_ref, l_ref, acc_ref, v)

    reset()

    def sel_body(j, c):
        sel_tile(j, False)
        return c

    lax.fori_loop(0, i, sel_body, 0)
    sel_tile(i, True)
    o_sel = acc_ref[...] / l_ref[...]

    def win_tile(j):
        k = kw_ref[pl.ds(j * tq, tq), :].astype(BF16)
        v = vw_ref[pl.ds(j * tq, tq), :].astype(BF16)
        kpos = j * tq + colk
        qp = i * tq + row
        keep = (kpos <= qp) & (kpos > qp - cfg.window)
        s = jnp.where(keep, _dot_t(q4, k), NEG)
        _softmax_step(s, m_ref, l_ref, acc_ref, v)

    reset()
    wt = -(-cfg.window // tq)

    def win_body(j, c):
        win_tile(j)
        return c

    lax.fori_loop(jnp.maximum(i - wt, 0), i + 1, win_body, 0)
    o_win = acc_ref[...] / l_ref[...]

    gates = jax.nn.sigmoid(gate_ref[...])
    for r in range(rep):
        h = gidx * rep + r
        lane = lax.broadcasted_iota(jnp.int32, gates.shape, 1)

        def gcol(c):
            return jnp.sum(jnp.where(lane == 3 * h + c, gates, 0.0), axis=-1, keepdims=True)

        o_ref[:, r * W:(r + 1) * W] = (gcol(0) * o_cmp[r] + gcol(1) * o_sel[r * tq:(r + 1) * tq]
                                       + gcol(2) * o_win[r * tq:(r + 1) * tq])


def nsa_prompt(cfg, bq_raw, bq_rot, cmp_kv, nsa_kv, win_kv, gates, bsz, n_t, tq=256):
    tq = _pick(n_t, tq)
    assert tq % cfg.cmp_block == 0
    nq = n_t // tq
    g, W = cfg.b_kv, LANES
    rep = cfg.b_heads // g
    nb = n_t // cfg.cmp_block
    qspec = pl.BlockSpec((tq, rep * W), lambda b, gi, i: (b * nq + i, gi))

    def kvspec(off):
        return pl.BlockSpec((n_t, W), lambda b, gi, i: (b, off + gi))

    return pl.pallas_call(
        functools.partial(_nsa_prompt_kernel, cfg=cfg, tq=tq, rep=rep, nb=nb),
        grid=(bsz, g, nq),
        in_specs=[qspec, qspec,
                  pl.BlockSpec((nb, 2 * g * W), lambda b, gi, i: (b, 0)),
                  kvspec(2 * g), kvspec(3 * g), kvspec(0), kvspec(g),
                  pl.BlockSpec((tq, W), lambda b, gi, i: (b * nq + i, 0))],
        out_specs=qspec,
        out_shape=jax.ShapeDtypeStruct((bsz * n_t, cfg.b_heads * W), F32),
        scratch_shapes=[pltpu.VMEM((rep * tq, 1), F32), pltpu.VMEM((rep * tq, 1), F32),
                        pltpu.VMEM((rep * tq, W), F32)],
        compiler_params=_cp(("parallel", "parallel", "arbitrary")),
        name="nsa_prompt",
    )(bq_raw, bq_rot, cmp_kv, nsa_kv, nsa_kv, win_kv, win_kv, gates)


def _softplus(z):
    return jnp.maximum(z, 0.0) + jnp.log1p(jnp.exp(-jnp.abs(z)))


def _sb_tile(q, k, v, upper, carry, mask):
    z = _dot_t(q, k)
    sp = _softplus(z)
    lm = -sp if mask is None else jnp.where(mask, -sp, 0.0)
    hi = lm.astype(BF16)
    lo = (lm - hi.astype(F32)).astype(BF16)
    between = (jnp.dot(hi, upper, preferred_element_type=F32)
               + jnp.dot(lo, upper, preferred_element_type=F32)) + carry
    w = jnp.exp(z - sp + between)
    if mask is not None:
        w = jnp.where(mask, w, 0.0)
    out = jnp.dot(w.astype(BF16), v, preferred_element_type=F32)
    return out, carry + jnp.sum(lm, axis=-1, keepdims=True)


def _sb_prompt_kernel(q_ref, k_ref, v_ref, o_ref, c_ref, acc_ref, *, tq):
    i = pl.program_id(2)
    q = (q_ref[...] * (LANES ** -0.5)).astype(BF16)
    r = lax.broadcasted_iota(jnp.int32, (tq, tq), 0)
    c = lax.broadcasted_iota(jnp.int32, (tq, tq), 1)
    upper = jnp.where(r > c, 1.0, 0.0).astype(BF16)
    k = k_ref[pl.ds(i * tq, tq), :].astype(BF16)
    v = v_ref[pl.ds(i * tq, tq), :].astype(BF16)
    out, carry = _sb_tile(q, k, v, upper, jnp.zeros((tq, 1), F32), c < r)
    acc_ref[...] = out
    c_ref[...] = carry

    def body(st):
        t, _ = st
        j = i - 1 - t
        kk = k_ref[pl.ds(j * tq, tq), :].astype(BF16)
        vv = v_ref[pl.ds(j * tq, tq), :].astype(BF16)
        o, cnew = _sb_tile(q, kk, vv, upper, c_ref[...], None)
        acc_ref[...] += o
        c_ref[...] = cnew
        return t + 1, jnp.max(cnew)

    lax.while_loop(lambda st: (st[0] < i) & (st[1] > SB_EXIT), body, (jnp.int32(0), jnp.max(carry)))
    o_ref[...] = acc_ref[...]


def sb_prompt(cfg, qkv, bsz, n_t, tq=256):
    tq = _pick(n_t, tq)
    nq = n_t // tq
    ch, W = cfg.c_heads, LANES
    return pl.pallas_call(
        functools.partial(_sb_prompt_kernel, tq=tq),
        grid=(bsz, ch, nq),
        in_specs=[pl.BlockSpec((tq, W), lambda b, h, i: (b * nq + i, h)),
                  pl.BlockSpec((n_t, W), lambda b, h, i: (b, ch + h)),
                  pl.BlockSpec((n_t, W), lambda b, h, i: (b, 2 * ch + h))],
        out_specs=pl.BlockSpec((tq, W), lambda b, h, i: (b * nq + i, h)),
        out_shape=jax.ShapeDtypeStruct((bsz * n_t, ch * W), F32),
        scratch_shapes=[pltpu.VMEM((tq, 1), F32), pltpu.VMEM((tq, W), F32)],
        compiler_params=_cp(("parallel", "parallel", "arbitrary")),
        name="sb_prompt",
    )(qkv, qkv, qkv)


def _router_kernel(h_ref, w_ref, o_ref, *, n_exp, top_k):
    logits = jnp.dot(h_ref[...], w_ref[...], preferred_element_type=F32, precision=lax.Precision.HIGHEST)
    idx = lax.broadcasted_iota(jnp.int32, logits.shape, 1)
    valid = idx < n_exp
    work = jnp.where(valid, logits, -jnp.inf)
    picked = jnp.zeros(logits.shape, jnp.bool_)
    for _ in range(top_k):
        mx = jnp.max(work, axis=-1, keepdims=True)
        first = jnp.min(jnp.where(work == mx, idx, logits.shape[-1]), axis=-1, keepdims=True)
        hit = idx == first
        picked = picked | hit
        work = jnp.where(hit, -jnp.inf, work)
    top = jnp.max(jnp.where(valid, logits, -jnp.inf), axis=-1, keepdims=True)
    e = jnp.where(picked, jnp.exp(logits - top), 0.0)
    o_ref[...] = e / jnp.sum(e, axis=-1, keepdims=True)


def router(cfg, h, w_router_pad, tm=512):
    m, d = h.shape
    tm = _pick(m, tm)
    return pl.pallas_call(
        functools.partial(_router_kernel, n_exp=cfg.n_exp, top_k=cfg.top_k),
        grid=(m // tm,),
        in_specs=[pl.BlockSpec((tm, d), lambda i: (i, 0)),
                  pl.BlockSpec((d, LANES), lambda i: (0, 0))],
        out_specs=pl.BlockSpec((tm, LANES), lambda i: (i, 0)),
        out_shape=jax.ShapeDtypeStruct((m, LANES), F32),
        compiler_params=_cp(("parallel",)),
        name="router",
    )(h, w_router_pad)


def _route_kernel(h_ref, w_ref, id_ref, gate_ref, *, n_exp, top_k):
    logits = jnp.dot(h_ref[...], w_ref[...], preferred_element_type=F32, precision=lax.Precision.HIGHEST)
    idx = lax.broadcasted_iota(jnp.int32, logits.shape, 1)
    work = jnp.where(idx < n_exp, logits, -jnp.inf)
    ids = jnp.zeros(logits.shape, jnp.int32)
    ex = jnp.zeros(logits.shape, F32)
    top = None
    for t in range(top_k):
        mx = jnp.max(work, axis=-1, keepdims=True)
        first = jnp.min(jnp.where(work == mx, idx, logits.shape[-1]), axis=-1, keepdims=True)
        top = mx if top is None else top
        ids = jnp.where(idx == t, first, ids)
        ex = jnp.where(idx == t, jnp.exp(mx - top), ex)
        work = jnp.where(idx == first, -jnp.inf, work)
    id_ref[...] = ids
    gate_ref[...] = ex / jnp.sum(ex, axis=-1, keepdims=True)


def route(cfg, h, w_router_pad, tm=512):
    m, d = h.shape
    tm = _pick(m, tm)
    return pl.pallas_call(
        functools.partial(_route_kernel, n_exp=cfg.n_exp, top_k=cfg.top_k),
        grid=(m // tm,),
        in_specs=[pl.BlockSpec((tm, d), lambda i: (i, 0)),
                  pl.BlockSpec((d, LANES), lambda i: (0, 0))],
        out_specs=[pl.BlockSpec((tm, LANES), lambda i: (i, 0))] * 2,
        out_shape=[jax.ShapeDtypeStruct((m, LANES), jnp.int32), jax.ShapeDtypeStruct((m, LANES), F32)],
        compiler_params=_cp(("parallel",)),
        name="route",
    )(h, w_router_pad)


GATHER_LAG = 32


def _row_gather_kernel(idx_ref, x_ref, o_ref, sem, *, n_rows):
    def copy(src, dst):
        return pltpu.make_async_copy(x_ref.at[pl.ds(src, 1)], o_ref.at[pl.ds(dst, 1)], sem)

    def body(r, c):
        copy(idx_ref[r], r).start()

        @pl.when(r >= GATHER_LAG)
        def _():
            copy(0, r - GATHER_LAG).wait()
        return c

    lax.fori_loop(0, n_rows, body, 0)

    def drain(r, c):
        copy(0, r).wait()
        return c

    lax.fori_loop(max(n_rows - GATHER_LAG, 0), n_rows, drain, 0)


def row_gather(x, idx):
    n_rows = idx.shape[0]
    return pl.pallas_call(
        functools.partial(_row_gather_kernel, n_rows=n_rows),
        grid_spec=pltpu.PrefetchScalarGridSpec(
            num_scalar_prefetch=1,
            grid=(1,),
            in_specs=[pl.BlockSpec(memory_space=pl.ANY)],
            out_specs=pl.BlockSpec(memory_space=pl.ANY),
            scratch_shapes=[pltpu.SemaphoreType.DMA(())]),
        out_shape=jax.ShapeDtypeStruct((n_rows, x.shape[1]), x.dtype),
        compiler_params=_cp(("arbitrary",)),
        name="row_gather",
    )(idx, x)


def _group_swiglu_kernel(te_ref, nu_ref, x_ref, wg_ref, wu_ref, o_ref):
    del te_ref
    t = pl.program_id(1)

    @pl.when(t < nu_ref[0])
    def _():
        _swiglu_kernel(x_ref, wg_ref, wu_ref, o_ref)

    @pl.when(t >= nu_ref[0])
    def _():
        o_ref[...] = jnp.zeros(o_ref.shape, o_ref.dtype)


def group_swiglu(x, w_gu, layer_i, tile_expert, n_used, d_ff, tm, tn=512):
    r, kdim = x.shape
    tn = _pick(d_ff, tn)
    nb = d_ff // tn
    wblk = (None, None, kdim, tn)
    return pl.pallas_call(
        _group_swiglu_kernel,
        grid_spec=pltpu.PrefetchScalarGridSpec(
            num_scalar_prefetch=2,
            grid=(nb, r // tm),
            in_specs=[pl.BlockSpec((tm, kdim), lambda j, t, te, nu: (t, 0)),
                      pl.BlockSpec(wblk, lambda j, t, te, nu: (layer_i, te[t], 0, j)),
                      pl.BlockSpec(wblk, lambda j, t, te, nu: (layer_i, te[t], 0, nb + j))],
            out_specs=pl.BlockSpec((tm, tn), lambda j, t, te, nu: (t, j))),
        out_shape=jax.ShapeDtypeStruct((r, d_ff), BF16),
        compiler_params=_cp(("arbitrary", "arbitrary")),
        name="group_swiglu",
    )(tile_expert, n_used, x, w_gu, w_gu)


def _group_down_kernel(te_ref, nu_ref, a_ref, w_ref, g_ref, o_ref):
    del te_ref
    t = pl.program_id(1)

    @pl.when(t < nu_ref[0])
    def _():
        y = jnp.dot(a_ref[...], w_ref[...].astype(BF16), preferred_element_type=F32)
        o_ref[...] = g_ref[...] * y

    @pl.when(t >= nu_ref[0])
    def _():
        o_ref[...] = jnp.zeros(o_ref.shape, o_ref.dtype)


def group_down(a, w_down, layer_i, tile_expert, n_used, row_gate, tm, tn=512):
    r, kdim = a.shape
    n = w_down.shape[-1]
    tn = _pick(n, tn)
    return pl.pallas_call(
        _group_down_kernel,
        grid_spec=pltpu.PrefetchScalarGridSpec(
            num_scalar_prefetch=2,
            grid=(n // tn, r // tm),
            in_specs=[pl.BlockSpec((tm, kdim), lambda j, t, te, nu: (t, 0)),
                      pl.BlockSpec((None, None, kdim, tn), lambda j, t, te, nu: (layer_i, te[t], 0, j)),
                      pl.BlockSpec((tm, 1), lambda j, t, te, nu: (t, 0))],
            out_specs=pl.BlockSpec((tm, tn), lambda j, t, te, nu: (t, j))),
        out_shape=jax.ShapeDtypeStruct((r, n), F32),
        compiler_params=_cp(("arbitrary", "arbitrary")),
        name="group_down",
    )(tile_expert, n_used, a, w_down, row_gate)


def _ln2_kernel(x_ref, f1_ref, f2_ref, g_ref, b_ref, o_ref, *, alpha, eps):
    y = alpha * x_ref[...] + (f1_ref[...] + f2_ref[...])
    mu = jnp.mean(y, axis=-1, keepdims=True)
    d = y - mu
    var = jnp.mean(d * d, axis=-1, keepdims=True)
    o_ref[...] = d * lax.rsqrt(var + eps) * g_ref[...] + b_ref[...]


def add2_ln(cfg, x, f, row1, row2, g, b, layer, tm=256):
    m, d = x.shape
    tm = _pick(m, tm)
    assert row1 % tm == 0 and row2 % tm == 0
    o1, o2 = row1 // tm, row2 // tm
    return pl.pallas_call(
        functools.partial(_ln2_kernel, alpha=cfg.alpha, eps=cfg.ln_eps),
        grid=(m // tm,),
        in_specs=[pl.BlockSpec((tm, d), lambda i: (i, 0)),
                  pl.BlockSpec((tm, d), lambda i: (o1 + i, 0)),
                  pl.BlockSpec((tm, d), lambda i: (o2 + i, 0)),
                  pl.BlockSpec((None, 1, d), lambda i: (layer, 0, 0)),
                  pl.BlockSpec((None, 1, d), lambda i: (layer, 0, 0))],
        out_specs=pl.BlockSpec((tm, d), lambda i: (i, 0)),
        out_shape=jax.ShapeDtypeStruct((m, d), F32),
        compiler_params=_cp(("parallel",)),
        name="add2_ln",
    )(x, f, f, g.reshape(-1, 1, d), b.reshape(-1, 1, d))


def moe_layer(cfg, xp, xs, moe_router, moe_w_gu, moe_w_down, layer_i, ln_g, ln_b, layer, tm=512):
    mp, d = xp.shape
    ms = xs.shape[0]
    m = mp + ms
    k, n_exp = cfg.top_k, cfg.n_exp
    assert k == 2
    w_r = jnp.pad(moe_router[layer_i], ((0, 0), (0, LANES - n_exp)))
    ids_p, gts_p = route(cfg, xp, w_r)
    ids_s, gts_s = route(cfg, xs, w_r)
    ids = jnp.concatenate([ids_p[:, :k], ids_s[:, :k]], axis=0).reshape(-1)
    gts = jnp.concatenate([gts_p[:, :k], gts_s[:, :k]], axis=0).reshape(-1)
    n_asg = m * k
    n_tiles = (n_asg + n_exp * (tm - 1)) // tm
    order = jnp.argsort(ids, stable=True).astype(jnp.int32)
    e_sorted = ids[order]
    counts = jnp.sum(ids[:, None] == jnp.arange(n_exp, dtype=jnp.int32)[None, :], axis=0).astype(jnp.int32)
    tiles_per = (counts + tm - 1) // tm
    tile_end = jnp.cumsum(tiles_per)
    pad_start = (tile_end - tiles_per) * tm
    grp_start = jnp.cumsum(counts) - counts
    dest_sorted = pad_start[e_sorted] + jnp.arange(n_asg, dtype=jnp.int32) - grp_start[e_sorted]
    n_used = tile_end[-1:].astype(jnp.int32)
    tile_ids = jnp.minimum(jnp.arange(n_tiles, dtype=jnp.int32), n_used[0] - 1)
    tile_expert = jnp.sum(tile_ids[:, None] >= tile_end[None, :], axis=1).astype(jnp.int32)
    rows = jnp.arange(n_tiles * tm, dtype=jnp.int32)
    row_e = tile_expert[rows // tm]
    rank = rows - pad_start[row_e]
    real = (rank < counts[row_e]) & (rows // tm < n_used[0])
    asg = order[jnp.clip(grp_start[row_e] + rank, 0, n_asg - 1)]
    src_row = jnp.where(real, asg // k, 0)
    row_gate = jnp.where(real, gts[asg], 0.0)
    dest = jnp.zeros((n_asg,), jnp.int32).at[order].set(dest_sorted).reshape(m, k)
    back = jnp.concatenate([dest[:mp, 0], dest[:mp, 1], dest[mp:, 0], dest[mp:, 1]])
    x_sorted = row_gather(jnp.concatenate([xp, xs], axis=0), src_row)
    a = group_swiglu(x_sorted, moe_w_gu, layer_i, tile_expert, n_used, cfg.d_ff, tm)
    y = group_down(a, moe_w_down, layer_i, tile_expert, n_used, row_gate[:, None], tm)
    yb = row_gather(y, back)
    xp_new = add2_ln(cfg, xp, yb, 0, mp, ln_g, ln_b, layer)
    xs_new = add2_ln(cfg, xs, yb, 2 * mp, 2 * mp + ms, ln_g, ln_b, layer)
    return xp_new, xs_new


def _moe_down_kernel(h_ref, w_ref, comb_ref, prev_ref, o_ref, acc_ref, *, nk, e):
    k = pl.program_id(2)
    part = jnp.dot(h_ref[...], w_ref[...].astype(BF16), preferred_element_type=F32)

    @pl.when(k == 0)
    def _():
        acc_ref[...] = part

    @pl.when(k > 0)
    def _():
        acc_ref[...] += part

    @pl.when(k == nk - 1)
    def _():
        comb = comb_ref[...]
        lane = lax.broadcasted_iota(jnp.int32, comb.shape, 1)
        ce = jnp.sum(jnp.where(lane == e, comb, 0.0), axis=-1, keepdims=True)
        o_ref[...] = prev_ref[...] + ce * acc_ref[...]


def moe_down_acc(h, w_down, lead, comb, prev, e, tm=1024, tn=512, tk=512):
    m, kdim = h.shape
    n = w_down.shape[-1]
    tm, tn, tk = _pick(m, tm), _pick(n, tn), _pick(kdim, tk)
    nk = kdim // tk
    return pl.pallas_call(
        functools.partial(_moe_down_kernel, nk=nk, e=e),
        grid=(m // tm, n // tn, nk),
        in_specs=[pl.BlockSpec((tm, tk), lambda i, j, k: (i, k)),
                  pl.BlockSpec((None, None, tk, tn), lambda i, j, k: tuple(lead) + (k, j)),
                  pl.BlockSpec((tm, LANES), lambda i, j, k: (i, 0)),
                  pl.BlockSpec((tm, tn), lambda i, j, k: (i, j))],
        out_specs=pl.BlockSpec((tm, tn), lambda i, j, k: (i, j)),
        out_shape=jax.ShapeDtypeStruct((m, n), F32),
        scratch_shapes=[pltpu.VMEM((tm, tn), F32)],
        input_output_aliases={3: 0},
        compiler_params=_cp(("parallel", "parallel", "arbitrary")),
        name="moe_down_acc",
    )(h, w_down, comb, prev)


def moe_dense(cfg, h, w_router_pad, moe_w_gu, moe_w_down, layer_i):
    comb = router(cfg, h, w_router_pad)
    out = jnp.zeros(h.shape, F32)
    for e in range(cfg.n_exp):
        a = swiglu_up(h, moe_w_gu, (layer_i, e), cfg.d_ff)
        out = moe_down_acc(a, moe_w_down, (layer_i, e), comb, out, e)
    return out


def _paged_spec(block, index_fn):
    return pl.BlockSpec(block, index_fn)


def _diff_sample_kernel(pt_ref, q_ref, new_ref, page_ref, lam_ref, g_ref, o_ref, m_ref, l_ref, acc_ref,
                        *, d_qk, lam_init, eps, n_pages):
    del pt_ref
    j = pl.program_id(1)
    q = q_ref[...] * (d_qk ** -0.5)
    lo = lax.broadcasted_iota(jnp.int32, q.shape, 1) < d_qk

    def scores(k):
        prod = k * q[None]
        s1 = jnp.sum(jnp.where(lo[None], prod, 0.0), axis=-1, keepdims=True)
        s2 = jnp.sum(jnp.where(lo[None], 0.0, prod), axis=-1, keepdims=True)
        return s1, s2

    @pl.when(j == 0)
    def _():
        s1, s2 = scores(new_ref[0][None])
        m_ref[0] = s1[0]
        m_ref[1] = s2[0]
        l_ref[...] = jnp.ones(l_ref.shape, F32)
        acc_ref[0] = new_ref[1]
        acc_ref[1] = new_ref[1]

    v = page_ref[:, 1]
    for t, s in enumerate(scores(page_ref[:, 0])):
        m_old = m_ref[t]
        m_new = jnp.maximum(m_old, jnp.max(s, axis=0))
        a = jnp.exp(m_old - m_new)
        p = jnp.exp(s - m_new[None])
        l_ref[t] = a * l_ref[t] + jnp.sum(p, axis=0)
        acc_ref[t] = a * acc_ref[t] + jnp.sum(p * v, axis=0)
        m_ref[t] = m_new

    @pl.when(j == n_pages - 1)
    def _():
        lam = _diff_lambda(lam_ref, lam_init)
        o_ref[...] = _diff_finish(acc_ref[0] / l_ref[0], acc_ref[1] / l_ref[1], lam, g_ref[...], lam_init, eps)


def diff_sample(cfg, aq, dkv, cache, page_table, lam_vec, subln_g, layer_i, lam_init):
    dbsz = aq.shape[0]
    n_pages = page_table.shape[1]
    ah, W = cfg.a_heads, LANES
    out = pl.pallas_call(
        functools.partial(_diff_sample_kernel, d_qk=cfg.a_qk, lam_init=lam_init, eps=cfg.rms_eps, n_pages=n_pages),
        grid_spec=pltpu.PrefetchScalarGridSpec(
            num_scalar_prefetch=1,
            grid=(dbsz, n_pages),
            in_specs=[pl.BlockSpec((None, ah, W), lambda b, j, pt: (b, 0, 0)),
                      pl.BlockSpec((None, 2, ah, W), lambda b, j, pt: (b, 0, 0, 0)),
                      pl.BlockSpec((None, None, cfg.page, 2, ah, W),
                                   lambda b, j, pt: (layer_i, pt[b, j], 0, 0, 0, 0)),
                      pl.BlockSpec((None, 4, cfg.a_qk), lambda b, j, pt: (layer_i, 0, 0)),
                      pl.BlockSpec((None, 1, W), lambda b, j, pt: (layer_i, 0, 0))],
            out_specs=pl.BlockSpec((None, ah, W), lambda b, j, pt: (b, 0, 0)),
            scratch_shapes=[pltpu.VMEM((2, ah, 1), F32), pltpu.VMEM((2, ah, 1), F32), pltpu.VMEM((2, ah, W), F32)]),
        out_shape=jax.ShapeDtypeStruct((dbsz, ah, W), F32),
        compiler_params=_cp(("parallel", "arbitrary")),
        name="diff_sample",
    )(page_table, aq.reshape(dbsz, ah, W), dkv.reshape(dbsz, 2, ah, W), cache, lam_vec, subln_g.reshape(-1, 1, W))
    return out.reshape(dbsz, ah * W)


def _suffix_sum(x):
    n = x.shape[0]
    sh = 1
    while sh < n:
        x = x + jnp.concatenate([x[sh:], jnp.zeros((sh,) + x.shape[1:], x.dtype)], axis=0)
        sh *= 2
    return x


def _sb_sample_kernel(pt_ref, q_ref, page_ref, o_ref, c_ref, acc_ref, *, n_pages):
    del pt_ref
    j = pl.program_id(1)

    @pl.when(j == 0)
    def _():
        c_ref[...] = jnp.zeros(c_ref.shape, F32)
        acc_ref[...] = jnp.zeros(acc_ref.shape, F32)

    q = q_ref[...] * (LANES ** -0.5)
    z = jnp.sum(page_ref[:, 0] * q[None], axis=-1, keepdims=True)
    sp = _softplus(z)
    inc = _suffix_sum(-sp)
    between = inc + sp + c_ref[...][None]
    w = jnp.exp(z - sp + between)
    acc_ref[...] += jnp.sum(w * page_ref[:, 1], axis=0)
    c_ref[...] += inc[0]

    @pl.when(j == n_pages - 1)
    def _():
        o_ref[...] = acc_ref[...]


def sb_sample(cfg, q, cache, page_table, layer_i):
    dbsz = q.shape[0]
    n_pages = page_table.shape[1]
    ch, W = cfg.c_heads, LANES
    out = pl.pallas_call(
        functools.partial(_sb_sample_kernel, n_pages=n_pages),
        grid_spec=pltpu.PrefetchScalarGridSpec(
            num_scalar_prefetch=1,
            grid=(dbsz, n_pages),
            in_specs=[pl.BlockSpec((None, ch, W), lambda b, j, pt: (b, 0, 0)),
                      pl.BlockSpec((None, None, cfg.page, 2, ch, W),
                                   lambda b, j, pt: (layer_i, pt[b, n_pages - 1 - j], 0, 0, 0, 0))],
            out_specs=pl.BlockSpec((None, ch, W), lambda b, j, pt: (b, 0, 0)),
            scratch_shapes=[pltpu.VMEM((ch, 1), F32), pltpu.VMEM((ch, W), F32)]),
        out_shape=jax.ShapeDtypeStruct((dbsz, ch, W), F32),
        compiler_params=_cp(("parallel", "arbitrary")),
        name="sb_sample",
    )(page_table, q.reshape(dbsz, ch, W), cache)
    return out.reshape(dbsz, ch * W)


def _nsa_cmp_sample_kernel(pt_ref, q_ref, page_ref, pe_ref, w_ref, o_ref, sel_ref, pool_ref,
                           *, cfg, n_pages, nb):
    del pt_ref
    j = pl.program_id(1)
    blk, g, W = cfg.cmp_block, cfg.b_kv, LANES
    rep = cfg.b_heads // g
    per_page = cfg.page // blk
    x = page_ref[...]
    for t in range(per_page):
        s = jnp.sum(x[t * blk:(t + 1) * blk], axis=0) * (1.0 / blk)
        for sl in range(2):
            for gg in range(g):
                pool_ref[sl, gg, pl.ds(j * per_page + t, 1), :] = s[sl, gg:gg + 1, :]

    @pl.when(j == n_pages - 1)
    def _():
        scale = W ** -0.5
        bidx = lax.broadcasted_iota(jnp.int32, (1, nb), 1)
        lane = lax.broadcasted_iota(jnp.int32, (1, W), 1)
        for gg in range(g):
            kv = []
            for sl in range(2):
                pe = jnp.mean(pe_ref[sl], axis=0, keepdims=True)
                kv.append(jnp.dot((pool_ref[sl, gg] + pe).astype(BF16), w_ref[sl].astype(BF16),
                                  preferred_element_type=F32))
            q = q_ref[gg * rep:(gg + 1) * rep, :]
            sc = _dot_t(q.astype(BF16), kv[0].astype(BF16)) * scale
            e = jnp.exp(sc - jnp.max(sc, axis=-1, keepdims=True))
            p = e / jnp.sum(e, axis=-1, keepdims=True)
            o_ref[gg * rep:(gg + 1) * rep, :] = jnp.dot(p.astype(BF16), kv[1].astype(BF16),
                                                        preferred_element_type=F32)
            imp = jnp.sum(p, axis=0, keepdims=True)
            forced = (bidx == 0) | (bidx > nb - cfg.n_local)
            work = jnp.where(forced, cfg.forced, imp)
            picks = jnp.zeros((1, W), jnp.int32)
            for t in range(cfg.top_n - 1):
                mx = jnp.max(work, axis=-1, keepdims=True)
                first = jnp.min(jnp.where(work == mx, bidx, nb), axis=-1, keepdims=True)
                picks = jnp.where(lane == t, first, picks)
                work = jnp.where(bidx == first, -jnp.inf, work)
            sel_ref[gg:gg + 1, :] = picks


def nsa_cmp_sample(cfg, bq_raw, cache, page_table, cmp_pe, cmp_w, layer_i):
    dbsz = bq_raw.shape[0]
    n_pages = page_table.shape[1]
    g, W, blk = cfg.b_kv, LANES, cfg.cmp_block
    nb = n_pages * cfg.page // blk
    assert nb >= cfg.top_n and cfg.page % blk == 0
    return pl.pallas_call(
        functools.partial(_nsa_cmp_sample_kernel, cfg=cfg, n_pages=n_pages, nb=nb),
        grid_spec=pltpu.PrefetchScalarGridSpec(
            num_scalar_prefetch=1,
            grid=(dbsz, n_pages),
            in_specs=[pl.BlockSpec((None, cfg.b_heads, W), lambda b, j, pt: (b, 0, 0)),
                      pl.BlockSpec((None, None, cfg.page, 2, g, W),
                                   lambda b, j, pt: (layer_i, pt[b, j], 0, 0, 0, 0)),
                      pl.BlockSpec((None, 2, blk, W), lambda b, j, pt: (layer_i, 0, 0, 0)),
                      pl.BlockSpec((None, 2, W, W), lambda b, j, pt: (layer_i, 0, 0, 0))],
            out_specs=[pl.BlockSpec((None, cfg.b_heads, W), lambda b, j, pt: (b, 0, 0)),
                       pl.BlockSpec((None, g, W), lambda b, j, pt: (b, 0, 0))],
            scratch_shapes=[pltpu.VMEM((2, g, nb, W), F32)]),
        out_shape=[jax.ShapeDtypeStruct((dbsz, cfg.b_heads, W), F32),
                   jax.ShapeDtypeStruct((dbsz, g, W), jnp.int32)],
        compiler_params=_cp(("parallel", "arbitrary")),
        name="nsa_cmp_sample",
    )(page_table, bq_raw.reshape(dbsz, cfg.b_heads, W), cache, cmp_pe, cmp_w)


def _nsa_sel_sample_kernel(pt_ref, sel_ref, q_ref, new_ref, blk0_ref, blk1_ref, win_ref, wnew_ref,
                           osel_ref, owin_ref, m_ref, l_ref, acc_ref, *, cfg, n_steps, keep):
    del pt_ref, sel_ref
    s_idx = pl.program_id(1)
    W = LANES
    rep = cfg.b_heads // cfg.b_kv
    scale = W ** -0.5

    @pl.when(s_idx == 0)
    def _():
        for r in range(rep):
            sc = jnp.sum(new_ref[2] * q_ref[r], axis=-1, keepdims=True) * scale
            m_ref[r] = sc
            l_ref[r] = jnp.ones(sc.shape, F32)
            acc_ref[r] = new_ref[3]

    sub = lax.broadcasted_iota(jnp.int32, blk0_ref.shape[:1] + blk0_ref.shape[2:], 1)
    kk = jnp.where(sub == 0, blk0_ref[:, 0], blk1_ref[:, 0])
    vv = jnp.where(sub == 0, blk0_ref[:, 1], blk1_ref[:, 1])
    for r in range(rep):
        s = jnp.sum(kk * q_ref[r][None], axis=-1, keepdims=True) * scale
        m_old = m_ref[r]
        m_new = jnp.maximum(m_old, jnp.max(s, axis=0))
        a = jnp.exp(m_old - m_new)
        p = jnp.exp(s - m_new[None])
        l_ref[r] = a * l_ref[r] + jnp.sum(p, axis=0)
        acc_ref[r] = a * acc_ref[r] + jnp.sum(p * vv, axis=0)
        m_ref[r] = m_new

    @pl.when(s_idx == n_steps - 1)
    def _():
        kw, vw = win_ref[:, 0], win_ref[:, 1]
        rowi = lax.broadcasted_iota(jnp.int32, (keep, cfg.b_kv, 1), 0)
        valid = rowi > keep - cfg.window
        for r in range(rep):
            osel_ref[r] = acc_ref[r] / l_ref[r]
            q = q_ref[r]
            s = jnp.where(valid, jnp.sum(kw * q[None], axis=-1, keepdims=True) * scale, NEG)
            sn = jnp.sum(wnew_ref[0] * q, axis=-1, keepdims=True) * scale
            mx = jnp.maximum(jnp.max(s, axis=0), sn)
            p = jnp.exp(s - mx[None])
            pn = jnp.exp(sn - mx)
            owin_ref[r] = (jnp.sum(p * vw, axis=0) + pn * wnew_ref[1]) / (jnp.sum(p, axis=0) + pn)


def nsa_sel_sample(cfg, bq_rot, nkv, wkv, cache, win_state, page_table, sel_idx, layer_i):
    dbsz = bq_rot.shape[0]
    g, W, blk = cfg.b_kv, LANES, cfg.cmp_block
    rep = cfg.b_heads // g
    per_page = cfg.page // blk
    keep = win_state.shape[2]
    n_steps = cfg.top_n - 1
    q = jnp.transpose(bq_rot.reshape(dbsz, g, rep, W), (0, 2, 1, 3))

    def blk_spec(gg):
        def index(b, s, pt, sel):
            bid = sel[(b * g + gg) * W + s]
            return (layer_i, pt[b, bid // per_page], bid % per_page, 1, 0, 0)
        return pl.BlockSpec((None, None, blk, 2, g, W), index)

    assert g == 2
    out_spec = pl.BlockSpec((None, rep, g, W), lambda b, s, pt, sel: (b, 0, 0, 0))
    return pl.pallas_call(
        functools.partial(_nsa_sel_sample_kernel, cfg=cfg, n_steps=n_steps, keep=keep),
        grid_spec=pltpu.PrefetchScalarGridSpec(
            num_scalar_prefetch=2,
            grid=(dbsz, n_steps),
            in_specs=[pl.BlockSpec((None, rep, g, W), lambda b, s, pt, sel: (b, 0, 0, 0)),
                      pl.BlockSpec((None, 4, g, W), lambda b, s, pt, sel: (b, 0, 0, 0)),
                      blk_spec(0), blk_spec(1),
                      pl.BlockSpec((None, None, keep, 2, g, W), lambda b, s, pt, sel: (layer_i, b, 0, 0, 0, 0)),
                      pl.BlockSpec((None, 2, g, W), lambda b, s, pt, sel: (b, 0, 0, 0))],
            out_specs=[out_spec, out_spec],
            scratch_shapes=[pltpu.VMEM((rep, g, 1), F32), pltpu.VMEM((rep, g, 1), F32),
                            pltpu.VMEM((rep, g, W), F32)]),
        out_shape=[jax.ShapeDtypeStruct((dbsz, rep, g, W), F32)] * 2,
        compiler_params=_cp(("parallel", "arbitrary")),
        name="nsa_sel_sample",
    )(page_table, sel_idx.reshape(-1), q, nkv.reshape(dbsz, 4, g, W), cache, cache, win_state,
      wkv.reshape(dbsz, 2, g, W))


def _gather_pages(pool, page_table):
    rows = pool[page_table]
    return rows.reshape((page_table.shape[0], page_table.shape[1] * pool.shape[1]) + pool.shape[2:])


def _diff_sample_jnp(cfg, aq, all_k, all_v, lam, subln_g, lam_init):
    b = aq.shape[0]
    q = aq.reshape(b, cfg.a_heads, 2, cfg.a_qk)
    k = all_k.reshape(b, -1, cfg.a_heads, 2, cfg.a_qk)
    sc = jnp.einsum('bhmd,bkhmd->bhmk', q, k, precision=lax.Precision.HIGHEST) * cfg.a_qk ** -0.5
    p = jax.nn.softmax(sc, axis=-1)
    w = p[:, :, 0] - lam * p[:, :, 1]
    o = jnp.einsum('bhk,bkhd->bhd', w, all_v, precision=lax.Precision.HIGHEST)
    o = o * lax.rsqrt(jnp.mean(o * o, axis=-1, keepdims=True) + cfg.rms_eps) * subln_g * (1.0 - lam_init)
    return o.reshape(b, -1)


def _forward(cfg, x_prompt, x_sample, cache_diff_kv, cache_nsa_kv, state_nsa_win, cache_sb_kv, page_table,
             even_w_in, even_w_out, diff_lambda, diff_subln_g, nsa_cmp_pe, nsa_cmp_w,
             odd_w_in, odd_w_out, ln1_g, ln1_b, ln2_g, ln2_b,
             ffn_w_gu, ffn_w_down, moe_router, moe_w_gu, moe_w_down):
    bsz, n_t, d = x_prompt.shape
    dbsz = x_sample.shape[0]
    past_len = page_table.shape[1] * cfg.page
    W = LANES
    xp = x_prompt.reshape(bsz * n_t, d)
    xs = x_sample.reshape(dbsz, d)
    tabs_p = rope_tables(cfg, jnp.arange(n_t, dtype=jnp.int32))
    tabs_s = rope_tables(cfg, jnp.full((dbsz,), past_len, dtype=jnp.int32))
    outs = {k: [] for k in ("diff_p", "diff_s", "nsa_p", "nsa_s", "win_p", "win_s", "sb_p", "sb_s")}
    n_gate = 3 * cfg.b_heads
    for layer in range(cfg.depth):
        i = layer // 2
        if layer % 2 == 0:
            lam_init = 0.8 - 0.6 * math.exp(-0.3 * layer)
            w_gate = jnp.pad(even_w_in[i][:, cfg.even_main:], ((0, 0), (0, W - n_gate)))
            proj = matmul(xp, even_w_in, (i,), 0, cfg.even_main)
            gates = matmul(xp, w_gate)
            aq, dkv, bqr, bqt, nkv, wkv = even_split(cfg, proj, tabs_p, n_t)
            a_o = diff_prompt(cfg, aq, dkv, diff_lambda, diff_subln_g, i, lam_init, bsz, n_t)
            ckv = cmp_prep(cfg, nkv, nsa_cmp_pe, nsa_cmp_w, i)
            b_o = nsa_prompt(cfg, bqr, bqt, ckv, nkv, wkv, gates, bsz, n_t)
            mp = matmul(jnp.concatenate([a_o, b_o], axis=-1), even_w_out, (i,))
            keep = min(cfg.window, n_t)
            outs["diff_p"].append(dkv.reshape(bsz, n_t, 2, cfg.a_heads, W))
            outs["nsa_p"].append(nkv.reshape(bsz, n_t, 4, cfg.b_kv, W))
            outs["win_p"].append(wkv.reshape(bsz, n_t, 2, cfg.b_kv, W)[:, n_t - keep:])
            ms, dks, nks, wsn = _even_sample(cfg, xs, even_w_in, w_gate, even_w_out, i, tabs_s, lam_init,
                                             cache_diff_kv, cache_nsa_kv, state_nsa_win, page_table,
                                             diff_lambda, diff_subln_g, nsa_cmp_pe, nsa_cmp_w, past_len)
            outs["diff_s"].append(dks)
            outs["nsa_s"].append(nks)
            outs["win_s"].append(wsn)
        else:
            qkv = matmul(xp, odd_w_in, (i,))
            o = sb_prompt(cfg, qkv, bsz, n_t)
            mp = matmul(o, odd_w_out, (i,))
            outs["sb_p"].append(qkv[:, cfg.c_heads * W:].reshape(bsz, n_t, 2, cfg.c_heads, W))
            ms, sks = _odd_sample(cfg, xs, odd_w_in, odd_w_out, i, cache_sb_kv, page_table)
            outs["sb_s"].append(sks)
        xp = add_ln(cfg, xp, mp, ln1_g, ln1_b, layer)
        xs = add_ln(cfg, xs, ms, ln1_g, ln1_b, layer)
        if layer % 2 == 0:
            fp = matmul(swiglu_up(xp, ffn_w_gu, (i,), cfg.d_ff), ffn_w_down, (i,), tm=512)
            fs = matmul(swiglu_up(xs, ffn_w_gu, (i,), cfg.d_ff), ffn_w_down, (i,), tm=512)
            xp = add_ln(cfg, xp, fp, ln2_g, ln2_b, layer)
            xs = add_ln(cfg, xs, fs, ln2_g, ln2_b, layer)
        else:
            xp, xs = moe_layer(cfg, xp, xs, moe_router, moe_w_gu, moe_w_down, i, ln2_g, ln2_b, layer)
    st = {k: jnp.stack(v) for k, v in outs.items()}
    return (xp.reshape(bsz, n_t, d), xs.reshape(dbsz, 1, d), st["diff_p"], st["diff_s"], st["nsa_p"], st["nsa_s"],
            st["win_p"], st["win_s"], st["sb_p"], st["sb_s"])


def _even_sample(cfg, xs, even_w_in, w_gate, even_w_out, i, tabs_s, lam_init,
                 cache_diff_kv, cache_nsa_kv, state_nsa_win, page_table,
                 diff_lambda, diff_subln_g, nsa_cmp_pe, nsa_cmp_w, past_len):
    dbsz = xs.shape[0]
    W, g = LANES, cfg.b_kv
    rep = cfg.b_heads // g
    assert past_len % cfg.cmp_block == 0
    proj = matmul(xs, even_w_in, (i,), 0, cfg.even_main)
    gates = jax.nn.sigmoid(matmul(xs, w_gate)[:, :3 * cfg.b_heads]).reshape(dbsz, cfg.b_heads, 3)
    aq, dkv, bqr, bqt, nkv, wkv = even_split(cfg, proj, tabs_s, dbsz)
    a_o = diff_sample(cfg, aq, dkv, cache_diff_kv, page_table, diff_lambda, diff_subln_g, i, lam_init)
    o_cmp, sel_idx = nsa_cmp_sample(cfg, bqr, cache_nsa_kv, page_table, nsa_cmp_pe, nsa_cmp_w, i)
    o_sel, o_win = nsa_sel_sample(cfg, bqt, nkv, wkv, cache_nsa_kv, state_nsa_win, page_table, sel_idx, i)
    o_sel = jnp.transpose(o_sel, (0, 2, 1, 3)).reshape(dbsz, cfg.b_heads, W)
    o_win = jnp.transpose(o_win, (0, 2, 1, 3)).reshape(dbsz, cfg.b_heads, W)
    b_o = gates[..., 0:1] * o_cmp + gates[..., 1:2] * o_sel + gates[..., 2:3] * o_win
    ms = matmul(jnp.concatenate([a_o, b_o.reshape(dbsz, -1)], axis=-1), even_w_out, (i,))
    keep = state_nsa_win.shape[2]
    win = jnp.concatenate([state_nsa_win[i], wkv.reshape(dbsz, 1, 2, g, W)], axis=1)
    return (ms, dkv.reshape(dbsz, 1, 2, cfg.a_heads, W), nkv.reshape(dbsz, 1, 4, g, W), win[:, 1:keep + 1])


def _odd_sample(cfg, xs, odd_w_in, odd_w_out, i, cache_sb_kv, page_table):
    dbsz = xs.shape[0]
    ch, W = cfg.c_heads, LANES
    qkv = matmul(xs, odd_w_in, (i,))
    o = sb_sample(cfg, qkv[:, :ch * W], cache_sb_kv, page_table, i)
    return matmul(o, odd_w_out, (i,)), qkv[:, ch * W:].reshape(dbsz, 1, 2, ch, W)


def _even_sample_jnp(cfg, xs, even_w_in, w_gate, even_w_out, i, tabs_s, lam_init,
                     cache_diff_kv, cache_nsa_kv, state_nsa_win, page_table,
                     diff_lambda, diff_subln_g, nsa_cmp_pe, nsa_cmp_w, past_len):
    dbsz = xs.shape[0]
    W, g = LANES, cfg.b_kv
    rep = cfg.b_heads // g
    hp = lax.Precision.HIGHEST
    proj = matmul(xs, even_w_in, (i,), 0, cfg.even_main)
    gates = jax.nn.sigmoid(matmul(xs, w_gate)[:, :3 * cfg.b_heads]).reshape(dbsz, cfg.b_heads, 3)
    aq, dkv, bqr, bqt, nkv, wkv = even_split(cfg, proj, tabs_s, dbsz)
    new_diff = dkv.reshape(dbsz, 1, 2, cfg.a_heads, W)
    new_nsa = nkv.reshape(dbsz, 1, 4, g, W)
    new_win = wkv.reshape(dbsz, 1, 2, g, W)
    all_diff = jnp.concatenate([_gather_pages(cache_diff_kv[i], page_table), new_diff], axis=1)
    all_nsa = jnp.concatenate([_gather_pages(cache_nsa_kv[i], page_table), new_nsa], axis=1)
    win = jnp.concatenate([state_nsa_win[i], new_win], axis=1)
    lv = diff_lambda[i]
    lam = jnp.exp(jnp.sum(lv[0] * lv[1])) - jnp.exp(jnp.sum(lv[2] * lv[3])) + lam_init
    a_o = _diff_sample_jnp(cfg, aq, all_diff[:, :, 0], all_diff[:, :, 1], lam, diff_subln_g[i], lam_init)
    blk = cfg.cmp_block
    seq_len = past_len + 1
    nb_c = seq_len // blk
    blocks = all_nsa[:, :nb_c * blk, 0:2].reshape(dbsz, nb_c, blk, 2, g, W)
    summ = jnp.mean(blocks + jnp.transpose(nsa_cmp_pe[i], (1, 0, 2))[:, :, None, :], axis=2)
    summ = jnp.einsum('bnjgd,jde->bnjge', summ, nsa_cmp_w[i], precision=hp)
    k_cmp, v_cmp = summ[:, :, 0], summ[:, :, 1]
    scale = W ** -0.5
    qg = bqr.reshape(dbsz, g, rep, W)
    sc = jnp.einsum('bgrd,bngd->bgrn', qg, k_cmp, precision=hp) * scale
    p_cmp = jax.nn.softmax(sc, axis=-1)
    o_cmp = jnp.einsum('bgrn,bngd->bgrd', p_cmp, v_cmp, precision=hp).reshape(dbsz, cfg.b_heads, W)
    nb_s = -(-seq_len // blk)
    imp = jnp.pad(jnp.sum(p_cmp, axis=2), ((0, 0), (0, 0), (0, nb_s - nb_c)))
    bidx = jnp.arange(nb_s, dtype=jnp.int32)
    cur = past_len // blk
    forced = (bidx == 0) | ((bidx <= cur) & (bidx > cur - cfg.n_local))
    score = jnp.where(bidx > cur, -jnp.inf, jnp.where(forced, cfg.forced, imp))
    _, sel_idx = lax.top_k(score, min(cfg.top_n, nb_s))
    sel = jnp.pad(all_nsa[:, :, 2:4], ((0, 0), (0, nb_s * blk - seq_len), (0, 0), (0, 0), (0, 0)))
    sel_blocks = jnp.transpose(sel.reshape(dbsz, nb_s, blk, 2, g, W), (0, 4, 1, 2, 3, 5))
    gath = sel_blocks[jnp.arange(dbsz)[:, None, None], jnp.arange(g)[None, :, None], sel_idx]
    gath = gath.reshape(dbsz, g, -1, 2, W)
    kpos = (sel_idx[..., None] * blk + jnp.arange(blk, dtype=jnp.int32)).reshape(dbsz, g, -1)
    qt = bqt.reshape(dbsz, g, rep, W)
    ssc = jnp.einsum('bgrd,bgkd->bgrk', qt, gath[..., 0, :], precision=hp) * scale
    ps = jax.nn.softmax(jnp.where((kpos <= past_len)[:, :, None], ssc, -jnp.inf), axis=-1)
    o_sel = jnp.einsum('bgrk,bgkd->bgrd', ps, gath[..., 1, :], precision=hp).reshape(dbsz, cfg.b_heads, W)
    keep = state_nsa_win.shape[2]
    kp = past_len - keep + jnp.arange(keep + 1, dtype=jnp.int32)
    wmask = kp > past_len - cfg.window
    wsc = jnp.einsum('bgrd,bkgd->bgrk', qt, win[:, :, 0], precision=hp) * scale
    pw = jax.nn.softmax(jnp.where(wmask, wsc, -jnp.inf), axis=-1)
    o_win = jnp.einsum('bgrk,bkgd->bgrd', pw, win[:, :, 1], precision=hp).reshape(dbsz, cfg.b_heads, W)
    b_o = gates[..., 0:1] * o_cmp + gates[..., 1:2] * o_sel + gates[..., 2:3] * o_win
    mixed = jnp.concatenate([a_o, b_o.reshape(dbsz, -1)], axis=-1)
    ms = matmul(mixed, even_w_out, (i,))
    return ms, new_diff, new_nsa, win[:, win.shape[1] - keep:]


def _odd_sample_jnp(cfg, xs, odd_w_in, odd_w_out, i, cache_sb_kv, page_table):
    dbsz = xs.shape[0]
    ch, W = cfg.c_heads, LANES
    hp = lax.Precision.HIGHEST
    qkv = matmul(xs, odd_w_in, (i,))
    q = qkv[:, :ch * W].reshape(dbsz, ch, W)
    new_kv = qkv[:, ch * W:].reshape(dbsz, 1, 2, ch, W)
    past = _gather_pages(cache_sb_kv[i], page_table)
    z = jnp.einsum('bhd,bkhd->bhk', q, past[:, :, 0], precision=hp) * W ** -0.5
    log1m = -jax.nn.softplus(z)
    between = lax.cumsum(log1m, axis=2, reverse=True) - log1m
    w = jnp.exp(jax.nn.log_sigmoid(z) + between)
    o = jnp.einsum('bhk,bkhd->bhd', w, past[:, :, 1], precision=hp).reshape(dbsz, ch * W)
    return matmul(o, odd_w_out, (i,)), new_kv


def kernel(x_prompt, x_sample, cache_diff_kv, cache_nsa_kv, state_nsa_win, cache_sb_kv, page_table, even_w_in, even_w_out, diff_lambda, diff_subln_g, nsa_cmp_pe, nsa_cmp_w, odd_w_in, odd_w_out, ln1_g, ln1_b, ln2_g, ln2_b, ffn_w_gu, ffn_w_down, moe_router, moe_w_gu, moe_w_down):
    return _forward(Cfg(), x_prompt, x_sample, cache_diff_kv, cache_nsa_kv, state_nsa_win, cache_sb_kv, page_table,
                    even_w_in, even_w_out, diff_lambda, diff_subln_g, nsa_cmp_pe, nsa_cmp_w,
                    odd_w_in, odd_w_out, ln1_g, ln1_b, ln2_g, ln2_b,
                    ffn_w_gu, ffn_w_down, moe_router, moe_w_gu, moe_w_down)
```

```python
import functools
import math
from typing import NamedTuple

import numpy as np
import jax
import jax.numpy as jnp
from jax import lax
from jax.experimental import pallas as pl
from jax.experimental.pallas import tpu as pltpu

F32 = jnp.float32
BF16 = jnp.bfloat16

LANES = 128
VMEM_LIMIT = 56 * 1024 * 1024
NEG = -1.0e30
SB_EXIT = -104.0


class Cfg(NamedTuple):
    d_model: int = 2048
    depth: int = 4
    page: int = 128
    a_heads: int = 8
    a_qk: int = 64
    b_heads: int = 8
    b_kv: int = 2
    cmp_block: int = 64
    top_n: int = 16
    n_local: int = 2
    window: int = 512
    forced: float = 1.0e4
    c_heads: int = 16
    d_ff: int = 5632
    n_exp: int = 8
    top_k: int = 2
    theta: float = 10000.0
    q_block: int = 128
    ln_eps: float = 1e-5
    rms_eps: float = 1e-5

    @property
    def alpha(self):
        return (2 * self.depth) ** 0.25

    @property
    def even_main(self):
        return 4 * self.a_heads * LANES + 6 * self.b_kv * LANES


def _cp(sem, vmem=VMEM_LIMIT):
    return pltpu.CompilerParams(dimension_semantics=sem, vmem_limit_bytes=vmem)


def _pick(n, pref):
    if n <= pref:
        return n
    t = pref
    while n % t:
        t //= 2
    return t


def _mm_kernel(x_ref, w_ref, o_ref, acc_ref, *, nk):
    part = jnp.dot(x_ref[...].astype(BF16), w_ref[...].astype(BF16), preferred_element_type=F32)
    if nk == 1:
        o_ref[...] = part.astype(o_ref.dtype)
        return
    k = pl.program_id(2)

    @pl.when(k == 0)
    def _():
        acc_ref[...] = part

    @pl.when(k > 0)
    def _():
        acc_ref[...] += part

    @pl.when(k == nk - 1)
    def _():
        o_ref[...] = acc_ref[...].astype(o_ref.dtype)


def _w_spec(w, lead, tk, tn, cb0):
    nlead = len(lead)
    return pl.BlockSpec((None,) * nlead + (tk, tn), lambda i, j, k: tuple(lead) + (k, cb0 + j))


def matmul(x, w, lead=(), col0=0, n=None, tm=1024, tn=512, tk=None, out_dtype=F32):
    m, kdim = x.shape
    n = w.shape[-1] - col0 if n is None else n
    tm = _pick(m, tm)
    tn = _pick(n, tn)
    tk = kdim if tk is None else _pick(kdim, tk)
    assert col0 % tn == 0 and m % tm == 0 and n % tn == 0 and kdim % tk == 0
    nk = kdim // tk
    return pl.pallas_call(
        functools.partial(_mm_kernel, nk=nk),
        grid=(m // tm, n // tn, nk),
        in_specs=[pl.BlockSpec((tm, tk), lambda i, j, k: (i, k)),
                  _w_spec(w, lead, tk, tn, col0 // tn)],
        out_specs=pl.BlockSpec((tm, tn), lambda i, j, k: (i, j)),
        out_shape=jax.ShapeDtypeStruct((m, n), out_dtype),
        scratch_shapes=[pltpu.VMEM((tm, tn), F32)],
        compiler_params=_cp(("parallel", "parallel", "arbitrary")),
        name="matmul",
    )(x, w)


def _swiglu_kernel(x_ref, wg_ref, wu_ref, o_ref):
    x = x_ref[...].astype(BF16)
    g = jnp.dot(x, wg_ref[...].astype(BF16), preferred_element_type=F32)
    u = jnp.dot(x, wu_ref[...].astype(BF16), preferred_element_type=F32)
    o_ref[...] = (g * jax.nn.sigmoid(g) * u).astype(o_ref.dtype)


def swiglu_up(x, w_gu, lead, d_ff, tm=1024, tn=512):
    m, kdim = x.shape
    tm = _pick(m, tm)
    tn = _pick(d_ff, tn)
    nb = d_ff // tn
    nlead = len(lead)
    wblk = (None,) * nlead + (kdim, tn)
    return pl.pallas_call(
        _swiglu_kernel,
        grid=(m // tm, nb),
        in_specs=[pl.BlockSpec((tm, kdim), lambda i, j: (i, 0)),
                  pl.BlockSpec(wblk, lambda i, j: tuple(lead) + (0, j)),
                  pl.BlockSpec(wblk, lambda i, j: tuple(lead) + (0, nb + j))],
        out_specs=pl.BlockSpec((tm, tn), lambda i, j: (i, j)),
        out_shape=jax.ShapeDtypeStruct((m, d_ff), BF16),
        compiler_params=_cp(("parallel", "parallel")),
        name="swiglu_up",
    )(x, w_gu, w_gu)


def _ln_kernel(x_ref, f_ref, g_ref, b_ref, o_ref, *, alpha, eps):
    y = alpha * x_ref[...] + f_ref[...].astype(F32)
    mu = jnp.mean(y, axis=-1, keepdims=True)
    d = y - mu
    var = jnp.mean(d * d, axis=-1, keepdims=True)
    o_ref[...] = d * lax.rsqrt(var + eps) * g_ref[...] + b_ref[...]


def add_ln(cfg, x, f, g, b, layer, tm=256):
    m, d = x.shape
    tm = _pick(m, tm)
    return pl.pallas_call(
        functools.partial(_ln_kernel, alpha=cfg.alpha, eps=cfg.ln_eps),
        grid=(m // tm,),
        in_specs=[pl.BlockSpec((tm, d), lambda i: (i, 0)),
                  pl.BlockSpec((tm, d), lambda i: (i, 0)),
                  pl.BlockSpec((None, 1, d), lambda i: (layer, 0, 0)),
                  pl.BlockSpec((None, 1, d), lambda i: (layer, 0, 0))],
        out_specs=pl.BlockSpec((tm, d), lambda i: (i, 0)),
        out_shape=jax.ShapeDtypeStruct((m, d), F32),
        compiler_params=_cp(("parallel",)),
        name="add_ln",
    )(x, f, g.reshape(-1, 1, d), b.reshape(-1, 1, d))


def rope_tables(cfg, pos):
    out = []
    for d in (cfg.a_qk, LANES):
        half = d // 2
        inv = cfg.theta ** (-jnp.arange(half, dtype=F32) / half)
        ang = pos.astype(F32)[:, None] * inv[None, :]
        c, s = jnp.cos(ang), jnp.sin(ang)
        reps = LANES // d
        out.append(jnp.tile(jnp.concatenate([c, c], axis=-1), (1, reps)))
        out.append(jnp.tile(jnp.concatenate([-s, s], axis=-1), (1, reps)))
    return out


def _rope(x, c, s, d):
    if d == LANES:
        partner = pltpu.roll(x, LANES // 2, 1)
    else:
        lane = lax.broadcasted_iota(jnp.int32, x.shape, 1)
        lo = (lane % d) < (d // 2)
        partner = jnp.where(lo, pltpu.roll(x, LANES - d // 2, 1), pltpu.roll(x, d // 2, 1))
    return x * c + partner * s


def _split_kernel(p_ref, c64_ref, s64_ref, c128_ref, s128_ref,
                  aq_ref, dkv_ref, bqr_ref, bqt_ref, nkv_ref, win_ref, *, cfg):
    c64, s64, c128, s128 = c64_ref[...], s64_ref[...], c128_ref[...], s128_ref[...]
    ah, g = cfg.a_heads, cfg.b_kv
    W = LANES

    def col(i):
        return p_ref[:, i * W:(i + 1) * W]

    for h in range(ah):
        aq_ref[:, h * W:(h + 1) * W] = _rope(col(h), c64, s64, cfg.a_qk)
        dkv_ref[:, h * W:(h + 1) * W] = _rope(col(ah + h), c64, s64, cfg.a_qk)
        dkv_ref[:, (ah + h) * W:(ah + h + 1) * W] = col(2 * ah + h)
    for h in range(cfg.b_heads):
        x = col(3 * ah + h)
        bqr_ref[:, h * W:(h + 1) * W] = x
        bqt_ref[:, h * W:(h + 1) * W] = _rope(x, c128, s128, W)
    base = 3 * ah + cfg.b_heads
    for s in range(4):
        for j in range(g):
            x = col(base + s * g + j)
            nkv_ref[:, (s * g + j) * W:(s * g + j + 1) * W] = _rope(x, c128, s128, W) if s == 2 else x
    base += 4 * g
    for s in range(2):
        for j in range(g):
            x = col(base + s * g + j)
            win_ref[:, (s * g + j) * W:(s * g + j + 1) * W] = _rope(x, c128, s128, W) if s == 0 else x


def even_split(cfg, proj, tabs, n_pos):
    m = proj.shape[0]
    tm = _pick(n_pos, 256)
    npb = n_pos // tm
    ah, g, W = cfg.a_heads, cfg.b_kv, LANES
    widths = (ah * W, 2 * ah * W, cfg.b_heads * W, cfg.b_heads * W, 4 * g * W, 2 * g * W)
    tab_spec = pl.BlockSpec((tm, W), lambda i: (i % npb, 0))
    return pl.pallas_call(
        functools.partial(_split_kernel, cfg=cfg),
        grid=(m // tm,),
        in_specs=[pl.BlockSpec((tm, proj.shape[1]), lambda i: (i, 0))] + [tab_spec] * 4,
        out_specs=[pl.BlockSpec((tm, w), lambda i: (i, 0)) for w in widths],
        out_shape=[jax.ShapeDtypeStruct((m, w), F32) for w in widths],
        compiler_params=_cp(("parallel",)),
        name="even_split",
    )(proj, *tabs)


def _softmax_reset(m_ref, l_ref, acc_ref):
    m_ref[...] = jnp.full(m_ref.shape, NEG, F32)
    l_ref[...] = jnp.zeros(l_ref.shape, F32)
    acc_ref[...] = jnp.zeros(acc_ref.shape, F32)


def _softmax_step(s, m_ref, l_ref, acc_ref, v):
    chunks = [s[:, c * LANES:(c + 1) * LANES] for c in range(s.shape[1] // LANES)]
    cmax = functools.reduce(jnp.maximum, chunks)
    m_old = m_ref[...]
    m_new = jnp.maximum(m_old, jnp.max(cmax, axis=-1, keepdims=True))
    a = jnp.exp(m_old - m_new)
    ps = [jnp.exp(c - m_new) for c in chunks]
    l_ref[...] = a * l_ref[...] + functools.reduce(jnp.add, ps)
    p = jnp.concatenate(ps, axis=1).astype(BF16)
    acc_ref[...] = a * acc_ref[...] + jnp.dot(p, v, preferred_element_type=F32)
    m_ref[...] = m_new


def _softmax_result(l_ref, acc_ref):
    return acc_ref[...] / jnp.sum(l_ref[...], axis=-1, keepdims=True)


def _dot_t(a, b):
    return lax.dot_general(a, b, (((1,), (1,)), ((), ())), preferred_element_type=F32)


def _diff_lambda(lam_ref, lam_init):
    lv = lam_ref[...]
    a = jnp.sum(lv[0:1] * lv[1:2], axis=-1, keepdims=True)
    b = jnp.sum(lv[2:3] * lv[3:4], axis=-1, keepdims=True)
    return jnp.exp(a) - jnp.exp(b) + lam_init


def _diff_finish(o1, o2, lam, g, lam_init, eps):
    o = o1 - lam * o2
    o = o * lax.rsqrt(jnp.mean(o * o, axis=-1, keepdims=True) + eps)
    return o * g * (1.0 - lam_init)


def _diff_prompt_kernel(q_ref, k_ref, v_ref, lam_ref, g_ref, o_ref, m_ref, l_ref, acc_ref,
                        *, tq, tk, d_qk, lam_init, eps):
    i = pl.program_id(2)
    q = q_ref[...] * (d_qk ** -0.5)
    lane = lax.broadcasted_iota(jnp.int32, q.shape, 1)
    q2 = jnp.concatenate([jnp.where(lane < d_qk, q, 0.0), jnp.where(lane >= d_qk, q, 0.0)], axis=0).astype(BF16)
    _softmax_reset(m_ref, l_ref, acc_ref)
    n_full = (i * tq) // tk

    def tile(j, masked):
        k = k_ref[pl.ds(pl.multiple_of(j * tk, tk), tk), :].astype(BF16)
        v = v_ref[pl.ds(pl.multiple_of(j * tk, tk), tk), :].astype(BF16)
        s = _dot_t(q2, k)
        if masked:
            qpos = i * tq + lax.broadcasted_iota(jnp.int32, s.shape, 0) % tq
            kpos = j * tk + lax.broadcasted_iota(jnp.int32, s.shape, 1)
            s = jnp.where(kpos <= qpos, s, NEG)
        _softmax_step(s, m_ref, l_ref, acc_ref, v)

    def body(j, c):
        tile(j, False)
        return c

    lax.fori_loop(0, n_full, body, 0)
    tile(n_full, True)
    o = _softmax_result(l_ref, acc_ref)
    lam = _diff_lambda(lam_ref, lam_init)
    o_ref[...] = _diff_finish(o[:tq], o[tq:], lam, g_ref[...], lam_init, eps)


def diff_prompt(cfg, aq, dkv, lam_vec, subln_g, layer_i, lam_init, bsz, n_t, tq=256, tk=512):
    tq = _pick(n_t, tq)
    tk = _pick(n_t, tk)
    assert tk % tq == 0
    nq = n_t // tq
    ah, W = cfg.a_heads, LANES
    return pl.pallas_call(
        functools.partial(_diff_prompt_kernel, tq=tq, tk=tk, d_qk=cfg.a_qk, lam_init=lam_init, eps=cfg.rms_eps),
        grid=(bsz, ah, nq),
        in_specs=[pl.BlockSpec((tq, W), lambda b, h, i: (b * nq + i, h)),
                  pl.BlockSpec((n_t, W), lambda b, h, i: (b, h)),
                  pl.BlockSpec((n_t, W), lambda b, h, i: (b, ah + h)),
                  pl.BlockSpec((None, 4, cfg.a_qk), lambda b, h, i: (layer_i, 0, 0)),
                  pl.BlockSpec((None, 1, W), lambda b, h, i: (layer_i, 0, 0))],
        out_specs=pl.BlockSpec((tq, W), lambda b, h, i: (b * nq + i, h)),
        out_shape=jax.ShapeDtypeStruct((bsz * n_t, ah * W), F32),
        scratch_shapes=[pltpu.VMEM((2 * tq, W), F32)] * 3,
        compiler_params=_cp(("parallel", "parallel", "arbitrary")),
        name="diff_prompt",
    )(aq, dkv, dkv, lam_vec, subln_g.reshape(-1, 1, W))


def _cmp_prep_kernel(kv_ref, pe_ref, w_ref, o_ref, *, blk, g):
    rows = kv_ref[...]
    tb = rows.shape[0] // blk
    r = lax.broadcasted_iota(jnp.int32, (tb, rows.shape[0]), 0)
    c = lax.broadcasted_iota(jnp.int32, (tb, rows.shape[0]), 1)
    pool = jnp.where(c // blk == r, 1.0 / blk, 0.0).astype(F32)
    mean = jnp.dot(pool, rows, preferred_element_type=F32, precision=lax.Precision.HIGHEST)
    for j in range(2):
        pe = jnp.mean(pe_ref[j], axis=0, keepdims=True)
        for gg in range(g):
            sl = slice((j * g + gg) * LANES, (j * g + gg + 1) * LANES)
            o_ref[:, sl] = jnp.dot((mean[:, sl] + pe).astype(BF16), w_ref[j].astype(BF16),
                                   preferred_element_type=F32)


def cmp_prep(cfg, nsa_kv, cmp_pe, cmp_w, layer_i):
    m = nsa_kv.shape[0]
    blk, g, W = cfg.cmp_block, cfg.b_kv, LANES
    nb = m // blk
    tb = _pick(nb, 8)
    return pl.pallas_call(
        functools.partial(_cmp_prep_kernel, blk=blk, g=g),
        grid=(nb // tb,),
        in_specs=[pl.BlockSpec((tb * blk, 2 * g * W), lambda i: (i, 0)),
                  pl.BlockSpec((None, 2, blk, W), lambda i: (layer_i, 0, 0, 0)),
                  pl.BlockSpec((None, 2, W, W), lambda i: (layer_i, 0, 0, 0))],
        out_specs=pl.BlockSpec((tb, 2 * g * W), lambda i: (i, 0)),
        out_shape=jax.ShapeDtypeStruct((nb, 2 * g * W), F32),
        compiler_params=_cp(("parallel",)),
        name="cmp_prep",
    )(nsa_kv, cmp_pe, cmp_w)


def _top_n_mask(score, n_sel):
    nb = score.shape[-1]
    idx = lax.broadcasted_iota(jnp.int32, score.shape, 1)
    sel = jnp.zeros(score.shape, F32)
    work = score
    for _ in range(n_sel):
        mx = jnp.max(work, axis=-1, keepdims=True)
        first = jnp.min(jnp.where(work == mx, idx, nb), axis=-1, keepdims=True)
        hit = idx == first
        sel = jnp.where(hit, 1.0, sel)
        work = jnp.where(hit, -jnp.inf, work)
    return sel


def _nsa_prompt_kernel(qr_ref, qt_ref, ckv_ref, ks_ref, vs_ref, kw_ref, vw_ref, gate_ref, o_ref,
                       m_ref, l_ref, acc_ref, *, cfg, tq, tk, wspan, rep, nb):
    gidx = pl.program_id(1)
    i = pl.program_id(2)
    W = LANES
    blk = cfg.cmp_block
    scale = W ** -0.5
    rows = rep * tq
    qpos = i * tq + lax.broadcasted_iota(jnp.int32, (tq, 1), 0)

    kc = ckv_ref[:, pl.ds(pl.multiple_of(gidx * W, W), W)]
    vc = ckv_ref[:, pl.ds(pl.multiple_of((cfg.b_kv + gidx) * W, W), W)]
    bidx = lax.broadcasted_iota(jnp.int32, (tq, nb), 1)
    cvalid = (bidx + 1) * blk <= qpos + 1
    imp = jnp.zeros((tq, nb), F32)
    o_cmp = []
    for r in range(rep):
        q = qr_ref[:, r * W:(r + 1) * W]
        sc = _dot_t(q.astype(BF16), kc.astype(BF16)) * scale
        sc = jnp.where(cvalid, sc, -jnp.inf)
        mx = jnp.max(sc, axis=-1, keepdims=True)
        mx = jnp.where(mx > -jnp.inf, mx, 0.0)
        e = jnp.exp(sc - mx)
        p = e / jnp.maximum(jnp.sum(e, axis=-1, keepdims=True), 1e-30)
        imp = imp + p
        o_cmp.append(jnp.dot(p.astype(BF16), vc.astype(BF16), preferred_element_type=F32))
    cur = qpos // blk
    forced = (bidx == 0) | ((bidx <= cur) & (bidx > cur - cfg.n_local))
    score = jnp.where(bidx > cur, -jnp.inf, jnp.where(forced, cfg.forced, imp))
    sel = _top_n_mask(score, min(cfg.top_n, nb)).astype(BF16)

    q4 = jnp.concatenate([qt_ref[:, r * W:(r + 1) * W] for r in range(rep)], axis=0)
    q4 = (q4 * scale).astype(BF16)
    bpt = tk // blk
    n_full = (i * tq) // tk

    def sel_tile(j, diag):
        k = ks_ref[pl.ds(pl.multiple_of(j * tk, tk), tk), :].astype(BF16)
        v = vs_ref[pl.ds(pl.multiple_of(j * tk, tk), tk), :].astype(BF16)
        er = lax.broadcasted_iota(jnp.int32, (nb, tk), 0)
        ec = lax.broadcasted_iota(jnp.int32, (nb, tk), 1)
        expand = jnp.where(er == j * bpt + ec // blk, 1.0, 0.0).astype(BF16)
        keep = jnp.dot(sel, expand, preferred_element_type=F32) > 0.5
        if diag:
            kpos = j * tk + lax.broadcasted_iota(jnp.int32, (tq, tk), 1)
            keep = keep & (kpos <= qpos)
        keep = jnp.concatenate([keep] * rep, axis=0)
        s = jnp.where(keep, _dot_t(q4, k), NEG)
        _softmax_step(s, m_ref, l_ref, acc_ref, v)

    _softmax_reset(m_ref, l_ref, acc_ref)

    def sel_body(j, c):
        sel_tile(j, False)
        return c

    lax.fori_loop(0, n_full, sel_body, 0)
    sel_tile(n_full, True)
    o_sel = _softmax_result(l_ref, acc_ref)

    start = pl.multiple_of(jnp.maximum((i + 1) * tq - wspan, 0), tq)
    kw = kw_ref[pl.ds(start, wspan), :].astype(BF16)
    vw = vw_ref[pl.ds(start, wspan), :].astype(BF16)
    kpos = start + lax.broadcasted_iota(jnp.int32, (tq, wspan), 1)
    keep = (kpos <= qpos) & (kpos > qpos - cfg.window)
    keep = jnp.concatenate([keep] * rep, axis=0)
    _softmax_reset(m_ref, l_ref, acc_ref)
    _softmax_step(jnp.where(keep, _dot_t(q4, kw), NEG), m_ref, l_ref, acc_ref, vw)
    o_win = _softmax_result(l_ref, acc_ref)

    gates = jax.nn.sigmoid(gate_ref[...])
    for r in range(rep):
        h = gidx * rep + r
        lane = lax.broadcasted_iota(jnp.int32, gates.shape, 1)

        def gcol(c):
            return jnp.sum(jnp.where(lane == 3 * h + c, gates, 0.0), axis=-1, keepdims=True)

        o_ref[:, r * W:(r + 1) * W] = (gcol(0) * o_cmp[r] + gcol(1) * o_sel[r * tq:(r + 1) * tq]
                                       + gcol(2) * o_win[r * tq:(r + 1) * tq])


def nsa_prompt(cfg, bq_raw, bq_rot, cmp_kv, nsa_kv, win_kv, gates, bsz, n_t, tq=256, tk=512):
    tq = _pick(n_t, tq)
    tk = _pick(n_t, tk)
    assert tq % cfg.cmp_block == 0 and tk % tq == 0 and cfg.window % tq == 0
    wspan = min(cfg.window + tq, n_t)
    nq = n_t // tq
    g, W = cfg.b_kv, LANES
    rep = cfg.b_heads // g
    nb = n_t // cfg.cmp_block
    qspec = pl.BlockSpec((tq, rep * W), lambda b, gi, i: (b * nq + i, gi))

    def kvspec(off):
        return pl.BlockSpec((n_t, W), lambda b, gi, i: (b, off + gi))

    return pl.pallas_call(
        functools.partial(_nsa_prompt_kernel, cfg=cfg, tq=tq, tk=tk, wspan=wspan, rep=rep, nb=nb),
        grid=(bsz, g, nq),
        in_specs=[qspec, qspec,
                  pl.BlockSpec((nb, 2 * g * W), lambda b, gi, i: (b, 0)),
                  kvspec(2 * g), kvspec(3 * g), kvspec(0), kvspec(g),
                  pl.BlockSpec((tq, W), lambda b, gi, i: (b * nq + i, 0))],
        out_specs=qspec,
        out_shape=jax.ShapeDtypeStruct((bsz * n_t, cfg.b_heads * W), F32),
        scratch_shapes=[pltpu.VMEM((rep * tq, W), F32)] * 3,
        compiler_params=_cp(("parallel", "parallel", "arbitrary")),
        name="nsa_prompt",
    )(bq_raw, bq_rot, cmp_kv, nsa_kv, nsa_kv, win_kv, win_kv, gates)


def _softplus(z):
    return jnp.maximum(z, 0.0) + jnp.log1p(jnp.exp(-jnp.abs(z)))


def _sb_tile(q, k, v, upper, carry, mask):
    z = _dot_t(q, k)
    sp = _softplus(z)
    lm = -sp if mask is None else jnp.where(mask, -sp, 0.0)
    hi = lm.astype(BF16)
    lo = (lm - hi.astype(F32)).astype(BF16)
    between = (jnp.dot(hi, upper, preferred_element_type=F32)
               + jnp.dot(lo, upper, preferred_element_type=F32)) + carry
    w = jnp.exp(z - sp + between)
    if mask is not None:
        w = jnp.where(mask, w, 0.0)
    out = jnp.dot(w.astype(BF16), v, preferred_element_type=F32)
    return out, carry + jnp.sum(lm, axis=-1, keepdims=True)


def _sb_prompt_kernel(q_ref, k_ref, v_ref, o_ref, c_ref, acc_ref, *, tq):
    i = pl.program_id(2)
    q = (q_ref[...] * (LANES ** -0.5)).astype(BF16)
    r = lax.broadcasted_iota(jnp.int32, (tq, tq), 0)
    c = lax.broadcasted_iota(jnp.int32, (tq, tq), 1)
    upper = jnp.where(r > c, 1.0, 0.0).astype(BF16)
    k = k_ref[pl.ds(i * tq, tq), :].astype(BF16)
    v = v_ref[pl.ds(i * tq, tq), :].astype(BF16)
    out, carry = _sb_tile(q, k, v, upper, jnp.zeros((tq, 1), F32), c < r)
    acc_ref[...] = out
    c_ref[...] = carry

    def body(st):
        t, _ = st
        j = i - 1 - t
        kk = k_ref[pl.ds(j * tq, tq), :].astype(BF16)
        vv = v_ref[pl.ds(j * tq, tq), :].astype(BF16)
        o, cnew = _sb_tile(q, kk, vv, upper, c_ref[...], None)
        acc_ref[...] += o
        c_ref[...] = cnew
        return t + 1, jnp.max(cnew)

    lax.while_loop(lambda st: (st[0] < i) & (st[1] > SB_EXIT), body, (jnp.int32(0), jnp.max(carry)))
    o_ref[...] = acc_ref[...]


def sb_prompt(cfg, qkv, bsz, n_t, tq=256):
    tq = _pick(n_t, tq)
    nq = n_t // tq
    ch, W = cfg.c_heads, LANES
    return pl.pallas_call(
        functools.partial(_sb_prompt_kernel, tq=tq),
        grid=(bsz, ch, nq),
        in_specs=[pl.BlockSpec((tq, W), lambda b, h, i: (b * nq + i, h)),
                  pl.BlockSpec((n_t, W), lambda b, h, i: (b, ch + h)),
                  pl.BlockSpec((n_t, W), lambda b, h, i: (b, 2 * ch + h))],
        out_specs=pl.BlockSpec((tq, W), lambda b, h, i: (b * nq + i, h)),
        out_shape=jax.ShapeDtypeStruct((bsz * n_t, ch * W), F32),
        scratch_shapes=[pltpu.VMEM((tq, 1), F32), pltpu.VMEM((tq, W), F32)],
        compiler_params=_cp(("parallel", "parallel", "arbitrary")),
        name="sb_prompt",
    )(qkv, qkv, qkv)


def _router_kernel(h_ref, w_ref, o_ref, *, n_exp, top_k):
    logits = jnp.dot(h_ref[...], w_ref[...], preferred_element_type=F32, precision=lax.Precision.HIGHEST)
    idx = lax.broadcasted_iota(jnp.int32, logits.shape, 1)
    valid = idx < n_exp
    work = jnp.where(valid, logits, -jnp.inf)
    picked = jnp.zeros(logits.shape, jnp.bool_)
    for _ in range(top_k):
        mx = jnp.max(work, axis=-1, keepdims=True)
        first = jnp.min(jnp.where(work == mx, idx, logits.shape[-1]), axis=-1, keepdims=True)
        hit = idx == first
        picked = picked | hit
        work = jnp.where(hit, -jnp.inf, work)
    top = jnp.max(jnp.where(valid, logits, -jnp.inf), axis=-1, keepdims=True)
    e = jnp.where(picked, jnp.exp(logits - top), 0.0)
    o_ref[...] = e / jnp.sum(e, axis=-1, keepdims=True)


def router(cfg, h, w_router_pad, tm=512):
    m, d = h.shape
    tm = _pick(m, tm)
    return pl.pallas_call(
        functools.partial(_router_kernel, n_exp=cfg.n_exp, top_k=cfg.top_k),
        grid=(m // tm,),
        in_specs=[pl.BlockSpec((tm, d), lambda i: (i, 0)),
                  pl.BlockSpec((d, LANES), lambda i: (0, 0))],
        out_specs=pl.BlockSpec((tm, LANES), lambda i: (i, 0)),
        out_shape=jax.ShapeDtypeStruct((m, LANES), F32),
        compiler_params=_cp(("parallel",)),
        name="router",
    )(h, w_router_pad)


def _route_kernel(h_ref, w_ref, id_ref, gate_ref, *, n_exp, top_k):
    logits = jnp.dot(h_ref[...], w_ref[...], preferred_element_type=F32, precision=lax.Precision.HIGHEST)
    idx = lax.broadcasted_iota(jnp.int32, logits.shape, 1)
    work = jnp.where(idx < n_exp, logits, -jnp.inf)
    ids = jnp.zeros(logits.shape, jnp.int32)
    ex = jnp.zeros(logits.shape, F32)
    top = None
    for t in range(top_k):
        mx = jnp.max(work, axis=-1, keepdims=True)
        first = jnp.min(jnp.where(work == mx, idx, logits.shape[-1]), axis=-1, keepdims=True)
        top = mx if top is None else top
        ids = jnp.where(idx == t, first, ids)
        ex = jnp.where(idx == t, jnp.exp(mx - top), ex)
        work = jnp.where(idx == first, -jnp.inf, work)
    id_ref[...] = ids
    gate_ref[...] = ex / jnp.sum(ex, axis=-1, keepdims=True)


def route(cfg, h, w_router_pad, tm=512):
    m, d = h.shape
    tm = _pick(m, tm)
    return pl.pallas_call(
        functools.partial(_route_kernel, n_exp=cfg.n_exp, top_k=cfg.top_k),
        grid=(m // tm,),
        in_specs=[pl.BlockSpec((tm, d), lambda i: (i, 0)),
                  pl.BlockSpec((d, LANES), lambda i: (0, 0))],
        out_specs=[pl.BlockSpec((tm, LANES), lambda i: (i, 0))] * 2,
        out_shape=[jax.ShapeDtypeStruct((m, LANES), jnp.int32), jax.ShapeDtypeStruct((m, LANES), F32)],
        compiler_params=_cp(("parallel",)),
        name="route",
    )(h, w_router_pad)


GATHER_LAG = 32


def _row_gather_kernel(idx_ref, x_ref, o_ref, sem, *, n_rows):
    def copy(src, dst):
        return pltpu.make_async_copy(x_ref.at[pl.ds(src, 1)], o_ref.at[pl.ds(dst, 1)], sem)

    def body(r, c):
        copy(idx_ref[r], r).start()

        @pl.when(r >= GATHER_LAG)
        def _():
            copy(0, r - GATHER_LAG).wait()
        return c

    lax.fori_loop(0, n_rows, body, 0)

    def drain(r, c):
        copy(0, r).wait()
        return c

    lax.fori_loop(max(n_rows - GATHER_LAG, 0), n_rows, drain, 0)


def row_gather(x, idx):
    n_rows = idx.shape[0]
    return pl.pallas_call(
        functools.partial(_row_gather_kernel, n_rows=n_rows),
        grid_spec=pltpu.PrefetchScalarGridSpec(
            num_scalar_prefetch=1,
            grid=(1,),
            in_specs=[pl.BlockSpec(memory_space=pl.ANY)],
            out_specs=pl.BlockSpec(memory_space=pl.ANY),
            scratch_shapes=[pltpu.SemaphoreType.DMA(())]),
        out_shape=jax.ShapeDtypeStruct((n_rows, x.shape[1]), x.dtype),
        compiler_params=_cp(("arbitrary",)),
        name="row_gather",
    )(idx, x)


def _group_swiglu_kernel(te_ref, nu_ref, x_ref, wg_ref, wu_ref, o_ref):
    del te_ref
    t = pl.program_id(1)

    @pl.when(t < nu_ref[0])
    def _():
        _swiglu_kernel(x_ref, wg_ref, wu_ref, o_ref)

    @pl.when(t >= nu_ref[0])
    def _():
        o_ref[...] = jnp.zeros(o_ref.shape, o_ref.dtype)


def group_swiglu(x, w_gu, layer_i, tile_expert, n_used, d_ff, tm, tn=512):
    r, kdim = x.shape
    tn = _pick(d_ff, tn)
    nb = d_ff // tn
    wblk = (None, None, kdim, tn)
    return pl.pallas_call(
        _group_swiglu_kernel,
        grid_spec=pltpu.PrefetchScalarGridSpec(
            num_scalar_prefetch=2,
            grid=(nb, r // tm),
            in_specs=[pl.BlockSpec((tm, kdim), lambda j, t, te, nu: (t, 0)),
                      pl.BlockSpec(wblk, lambda j, t, te, nu: (layer_i, te[t], 0, j)),
                      pl.BlockSpec(wblk, lambda j, t, te, nu: (layer_i, te[t], 0, nb + j))],
            out_specs=pl.BlockSpec((tm, tn), lambda j, t, te, nu: (t, j))),
        out_shape=jax.ShapeDtypeStruct((r, d_ff), BF16),
        compiler_params=_cp(("arbitrary", "arbitrary")),
        name="group_swiglu",
    )(tile_expert, n_used, x, w_gu, w_gu)


def _group_down_kernel(te_ref, nu_ref, a_ref, w_ref, g_ref, hi_ref, lo_ref):
    del te_ref
    t = pl.program_id(1)

    @pl.when(t < nu_ref[0])
    def _():
        y = g_ref[...] * jnp.dot(a_ref[...], w_ref[...].astype(BF16), preferred_element_type=F32)
        hi = y.astype(BF16)
        hi_ref[...] = hi
        lo_ref[...] = (y - hi.astype(F32)).astype(BF16)

    @pl.when(t >= nu_ref[0])
    def _():
        hi_ref[...] = jnp.zeros(hi_ref.shape, hi_ref.dtype)
        lo_ref[...] = jnp.zeros(lo_ref.shape, lo_ref.dtype)


def group_down(a, w_down, layer_i, tile_expert, n_used, row_gate, tm, tn=512):
    r, kdim = a.shape
    n = w_down.shape[-1]
    tn = _pick(n, tn)
    out_spec = pl.BlockSpec((tm, tn), lambda j, t, te, nu: (t, j))
    return pl.pallas_call(
        _group_down_kernel,
        grid_spec=pltpu.PrefetchScalarGridSpec(
            num_scalar_prefetch=2,
            grid=(n // tn, r // tm),
            in_specs=[pl.BlockSpec((tm, kdim), lambda j, t, te, nu: (t, 0)),
                      pl.BlockSpec((None, None, kdim, tn), lambda j, t, te, nu: (layer_i, te[t], 0, j)),
                      pl.BlockSpec((tm, 1), lambda j, t, te, nu: (t, 0))],
            out_specs=[out_spec, out_spec]),
        out_shape=[jax.ShapeDtypeStruct((r, n), BF16)] * 2,
        compiler_params=_cp(("arbitrary", "arbitrary")),
        name="group_down",
    )(tile_expert, n_used, a, w_down, row_gate)


def _dispatch_kernel(c0_ref, nc_ref, src_ref, xp_ref, xs_ref, o_ref, acc_ref, *, ck, mp, n_ck):
    t, c = pl.program_id(0), pl.program_id(1)
    src = src_ref[...]

    def onehot(base, width):
        return jnp.where(src == base + lax.broadcasted_iota(jnp.int32, (1, width), 1), 1.0, 0.0).astype(BF16)

    @pl.when(c == 0)
    def _():
        acc_ref[...] = jnp.dot(onehot(mp, xs_ref.shape[0]), xs_ref[...].astype(BF16), preferred_element_type=F32)

    @pl.when(c < nc_ref[t])
    def _():
        acc_ref[...] += jnp.dot(onehot((c0_ref[t] + c) * ck, ck), xp_ref[...].astype(BF16),
                                preferred_element_type=F32)

    @pl.when(c == n_ck - 1)
    def _():
        o_ref[...] = acc_ref[...].astype(o_ref.dtype)


def dispatch_rows(xp, xs_pad, src_row, chunk0, n_chunk, tm, ck=512):
    mp, d = xp.shape
    r = src_row.shape[0]
    ck = _pick(mp, ck)
    n_ck = mp // ck
    return pl.pallas_call(
        functools.partial(_dispatch_kernel, ck=ck, mp=mp, n_ck=n_ck),
        grid_spec=pltpu.PrefetchScalarGridSpec(
            num_scalar_prefetch=2,
            grid=(r // tm, n_ck),
            in_specs=[pl.BlockSpec((tm, 1), lambda t, c, c0, nc: (t, 0)),
                      pl.BlockSpec((ck, d), lambda t, c, c0, nc:
                                   (c0[t] + jnp.minimum(c, jnp.maximum(nc[t] - 1, 0)), 0)),
                      pl.BlockSpec(xs_pad.shape, lambda t, c, c0, nc: (0, 0))],
            out_specs=pl.BlockSpec((tm, d), lambda t, c, c0, nc: (t, 0)),
            scratch_shapes=[pltpu.VMEM((tm, d), F32)]),
        out_shape=jax.ShapeDtypeStruct((r, d), BF16),
        compiler_params=_cp(("arbitrary", "arbitrary")),
        name="dispatch_rows",
    )(chunk0, n_chunk, src_row[:, None], xp, xs_pad)


def _combine_kernel(clo_ref, chi_ref, d_ref, x_ref, hi_ref, lo_ref, g_ref, b_ref, o_ref, acc_ref,
                    *, ck, n_exp, n_c, alpha, eps):
    tt, e, c = pl.program_id(0), pl.program_id(1), pl.program_id(2)

    @pl.when((e == 0) & (c == 0))
    def _():
        acc_ref[...] = jnp.zeros(acc_ref.shape, F32)

    chunk = clo_ref[tt * n_exp + e] + c

    @pl.when(chunk <= chi_ref[tt * n_exp + e])
    def _():
        rows = chunk * ck + lax.broadcasted_iota(jnp.int32, (1, ck), 1)
        dd = d_ref[...]
        p = jnp.where((dd[:, 0:1] == rows) | (dd[:, 1:2] == rows), 1.0, 0.0).astype(BF16)
        acc_ref[...] += (jnp.dot(p, hi_ref[...], preferred_element_type=F32)
                         + jnp.dot(p, lo_ref[...], preferred_element_type=F32))

    @pl.when((e == n_exp - 1) & (c == n_c - 1))
    def _():
        y = alpha * x_ref[...] + acc_ref[...]
        mu = jnp.mean(y, axis=-1, keepdims=True)
        dv = y - mu
        var = jnp.mean(dv * dv, axis=-1, keepdims=True)
        o_ref[...] = dv * lax.rsqrt(var + eps) * g_ref[...] + b_ref[...]


def combine_ln(cfg, x, dest, y_hi, y_lo, clo, chi, g, b, layer, tm, ck, n_c):
    m, d = x.shape
    n_exp = cfg.n_exp

    def ymap(tt, e, c, lo, hi):
        s = tt * n_exp + e
        return (jnp.maximum(jnp.minimum(lo[s] + c, hi[s]), 0), 0)

    return pl.pallas_call(
        functools.partial(_combine_kernel, ck=ck, n_exp=n_exp, n_c=n_c, alpha=cfg.alpha, eps=cfg.ln_eps),
        grid_spec=pltpu.PrefetchScalarGridSpec(
            num_scalar_prefetch=2,
            grid=(m // tm, n_exp, n_c),
            in_specs=[pl.BlockSpec((tm, 2), lambda tt, e, c, lo, hi: (tt, 0)),
                      pl.BlockSpec((tm, d), lambda tt, e, c, lo, hi: (tt, 0)),
                      pl.BlockSpec((ck, d), ymap), pl.BlockSpec((ck, d), ymap),
                      pl.BlockSpec((None, 1, d), lambda tt, e, c, lo, hi: (layer, 0, 0)),
                      pl.BlockSpec((None, 1, d), lambda tt, e, c, lo, hi: (layer, 0, 0))],
            out_specs=pl.BlockSpec((tm, d), lambda tt, e, c, lo, hi: (tt, 0)),
            scratch_shapes=[pltpu.VMEM((tm, d), F32)]),
        out_shape=jax.ShapeDtypeStruct((m, d), F32),
        compiler_params=_cp(("arbitrary", "arbitrary", "arbitrary")),
        name="combine_ln",
    )(clo, chi, dest, x, y_hi, y_lo, g.reshape(-1, 1, d), b.reshape(-1, 1, d))


def _ln2_kernel(x_ref, f1_ref, f2_ref, g_ref, b_ref, o_ref, *, alpha, eps):
    y = alpha * x_ref[...] + (f1_ref[...] + f2_ref[...])
    mu = jnp.mean(y, axis=-1, keepdims=True)
    d = y - mu
    var = jnp.mean(d * d, axis=-1, keepdims=True)
    o_ref[...] = d * lax.rsqrt(var + eps) * g_ref[...] + b_ref[...]


def add2_ln(cfg, x, f, row1, row2, g, b, layer, tm=256):
    m, d = x.shape
    tm = _pick(m, tm)
    assert row1 % tm == 0 and row2 % tm == 0
    o1, o2 = row1 // tm, row2 // tm
    return pl.pallas_call(
        functools.partial(_ln2_kernel, alpha=cfg.alpha, eps=cfg.ln_eps),
        grid=(m // tm,),
        in_specs=[pl.BlockSpec((tm, d), lambda i: (i, 0)),
                  pl.BlockSpec((tm, d), lambda i: (o1 + i, 0)),
                  pl.BlockSpec((tm, d), lambda i: (o2 + i, 0)),
                  pl.BlockSpec((None, 1, d), lambda i: (layer, 0, 0)),
                  pl.BlockSpec((None, 1, d), lambda i: (layer, 0, 0))],
        out_specs=pl.BlockSpec((tm, d), lambda i: (i, 0)),
        out_shape=jax.ShapeDtypeStruct((m, d), F32),
        compiler_params=_cp(("parallel",)),
        name="add2_ln",
    )(x, f, f, g.reshape(-1, 1, d), b.reshape(-1, 1, d))


def moe_layer(cfg, xp, xs, moe_router, moe_w_gu, moe_w_down, layer_i, ln_g, ln_b, layer, tm=512):
    mp, d = xp.shape
    ms = xs.shape[0]
    m = mp + ms
    k, n_exp = cfg.top_k, cfg.n_exp
    assert k == 2
    w_r = jnp.pad(moe_router[layer_i], ((0, 0), (0, LANES - n_exp)))
    ids_p, gts_p = route(cfg, xp, w_r)
    ids_s, gts_s = route(cfg, xs, w_r)
    ids = jnp.concatenate([ids_p[:, :k], ids_s[:, :k]], axis=0).reshape(-1)
    gts = jnp.concatenate([gts_p[:, :k], gts_s[:, :k]], axis=0).reshape(-1)
    n_asg = m * k
    n_tiles = (n_asg + n_exp * (tm - 1)) // tm
    order = jnp.argsort(ids, stable=True).astype(jnp.int32)
    e_sorted = ids[order]
    counts = jnp.sum(ids[:, None] == jnp.arange(n_exp, dtype=jnp.int32)[None, :], axis=0).astype(jnp.int32)
    tiles_per = (counts + tm - 1) // tm
    tile_end = jnp.cumsum(tiles_per)
    pad_start = (tile_end - tiles_per) * tm
    grp_start = jnp.cumsum(counts) - counts
    dest_sorted = pad_start[e_sorted] + jnp.arange(n_asg, dtype=jnp.int32) - grp_start[e_sorted]
    n_used = tile_end[-1:].astype(jnp.int32)
    tile_ids = jnp.minimum(jnp.arange(n_tiles, dtype=jnp.int32), n_used[0] - 1)
    tile_expert = jnp.sum(tile_ids[:, None] >= tile_end[None, :], axis=1).astype(jnp.int32)
    rows = jnp.arange(n_tiles * tm, dtype=jnp.int32)
    row_e = tile_expert[rows // tm]
    rank = rows - pad_start[row_e]
    real = (rank < counts[row_e]) & (rows // tm < n_used[0])
    asg = order[jnp.clip(grp_start[row_e] + rank, 0, n_asg - 1)]
    src_row = jnp.where(real, asg // k, -1)
    row_gate = jnp.where(real, gts[asg], 0.0)
    dest = jnp.zeros((n_asg,), jnp.int32).at[order].set(dest_sorted).reshape(m, k)
    ck = _pick(mp, 512)
    big = mp // ck
    in_p = (src_row >= 0) & (src_row < mp)
    src_ck = (src_row // ck).reshape(n_tiles, tm)
    c_min = jnp.min(jnp.where(in_p.reshape(n_tiles, tm), src_ck, big), axis=1)
    c_max = jnp.max(jnp.where(in_p.reshape(n_tiles, tm), src_ck, -1), axis=1)
    n_chunk = jnp.maximum(c_max - c_min + 1, 0).astype(jnp.int32)
    chunk0 = jnp.where(n_chunk > 0, c_min, 0).astype(jnp.int32)
    xs_pad = jnp.pad(xs, ((0, LANES - ms), (0, 0)))
    x_sorted = dispatch_rows(xp, xs_pad, src_row, chunk0, n_chunk, tm, ck)
    a = group_swiglu(x_sorted, moe_w_gu, layer_i, tile_expert, n_used, cfg.d_ff, tm)
    y_hi, y_lo = group_down(a, moe_w_down, layer_i, tile_expert, n_used, row_gate[:, None], tm)
    tmt = _pick(mp, tm)
    ck2 = tm // 2
    n_c = tmt // ck2 + 1
    per_tok = jnp.sum(ids.reshape(m, k)[:, :, None] == jnp.arange(n_exp, dtype=jnp.int32)[None, None, :], axis=1)
    before = jnp.concatenate([jnp.zeros((1, n_exp), jnp.int32), jnp.cumsum(per_tok, axis=0).astype(jnp.int32)])

    def chunk_range(t0, t1):
        lo_r, hi_r = before[t0], before[t1] - 1
        lo = (pad_start + lo_r) // ck2
        hi = (pad_start + hi_r) // ck2
        empty = hi_r < lo_r
        return jnp.where(empty, 1, lo).astype(jnp.int32), jnp.where(empty, 0, hi).astype(jnp.int32)

    edges = jnp.arange(0, mp + 1, tmt, dtype=jnp.int32)
    clo_p, chi_p = chunk_range(edges[:-1], edges[1:])
    clo_s, chi_s = chunk_range(jnp.array([mp]), jnp.array([m]))
    assert ms <= ck2
    xp_new = combine_ln(cfg, xp, dest[:mp], y_hi, y_lo, clo_p.reshape(-1), chi_p.reshape(-1),
                        ln_g, ln_b, layer, tmt, ck2, n_c)
    xs_new = combine_ln(cfg, xs, dest[mp:], y_hi, y_lo, clo_s.reshape(-1), chi_s.reshape(-1),
                        ln_g, ln_b, layer, ms, ck2, n_c)
    return xp_new, xs_new


def _moe_down_kernel(h_ref, w_ref, comb_ref, prev_ref, o_ref, acc_ref, *, nk, e):
    k = pl.program_id(2)
    part = jnp.dot(h_ref[...], w_ref[...].astype(BF16), preferred_element_type=F32)

    @pl.when(k == 0)
    def _():
        acc_ref[...] = part

    @pl.when(k > 0)
    def _():
        acc_ref[...] += part

    @pl.when(k == nk - 1)
    def _():
        comb = comb_ref[...]
        lane = lax.broadcasted_iota(jnp.int32, comb.shape, 1)
        ce = jnp.sum(jnp.where(lane == e, comb, 0.0), axis=-1, keepdims=True)
        o_ref[...] = prev_ref[...] + ce * acc_ref[...]


def moe_down_acc(h, w_down, lead, comb, prev, e, tm=1024, tn=512, tk=512):
    m, kdim = h.shape
    n = w_down.shape[-1]
    tm, tn, tk = _pick(m, tm), _pick(n, tn), _pick(kdim, tk)
    nk = kdim // tk
    return pl.pallas_call(
        functools.partial(_moe_down_kernel, nk=nk, e=e),
        grid=(m // tm, n // tn, nk),
        in_specs=[pl.BlockSpec((tm, tk), lambda i, j, k: (i, k)),
                  pl.BlockSpec((None, None, tk, tn), lambda i, j, k: tuple(lead) + (k, j)),
                  pl.BlockSpec((tm, LANES), lambda i, j, k: (i, 0)),
                  pl.BlockSpec((tm, tn), lambda i, j, k: (i, j))],
        out_specs=pl.BlockSpec((tm, tn), lambda i, j, k: (i, j)),
        out_shape=jax.ShapeDtypeStruct((m, n), F32),
        scratch_shapes=[pltpu.VMEM((tm, tn), F32)],
        input_output_aliases={3: 0},
        compiler_params=_cp(("parallel", "parallel", "arbitrary")),
        name="moe_down_acc",
    )(h, w_down, comb, prev)


def moe_dense(cfg, h, w_router_pad, moe_w_gu, moe_w_down, layer_i):
    comb = router(cfg, h, w_router_pad)
    out = jnp.zeros(h.shape, F32)
    for e in range(cfg.n_exp):
        a = swiglu_up(h, moe_w_gu, (layer_i, e), cfg.d_ff)
        out = moe_down_acc(a, moe_w_down, (layer_i, e), comb, out, e)
    return out


def _paged_spec(block, index_fn):
    return pl.BlockSpec(block, index_fn)


def _diff_sample_kernel(pt_ref, q_ref, new_ref, page_ref, lam_ref, g_ref, o_ref, m_ref, l_ref, acc_ref,
                        *, d_qk, lam_init, eps, n_pages):
    del pt_ref
    j = pl.program_id(1)
    q = q_ref[...] * (d_qk ** -0.5)
    lo = lax.broadcasted_iota(jnp.int32, q.shape, 1) < d_qk

    def scores(k):
        prod = k * q[None]
        s1 = jnp.sum(jnp.where(lo[None], prod, 0.0), axis=-1, keepdims=True)
        s2 = jnp.sum(jnp.where(lo[None], 0.0, prod), axis=-1, keepdims=True)
        return s1, s2

    @pl.when(j == 0)
    def _():
        s1, s2 = scores(new_ref[0][None])
        m_ref[0] = s1[0]
        m_ref[1] = s2[0]
        l_ref[...] = jnp.ones(l_ref.shape, F32)
        acc_ref[0] = new_ref[1]
        acc_ref[1] = new_ref[1]

    v = page_ref[:, 1]
    for t, s in enumerate(scores(page_ref[:, 0])):
        m_old = m_ref[t]
        m_new = jnp.maximum(m_old, jnp.max(s, axis=0))
        a = jnp.exp(m_old - m_new)
        p = jnp.exp(s - m_new[None])
        l_ref[t] = a * l_ref[t] + jnp.sum(p, axis=0)
        acc_ref[t] = a * acc_ref[t] + jnp.sum(p * v, axis=0)
        m_ref[t] = m_new

    @pl.when(j == n_pages - 1)
    def _():
        lam = _diff_lambda(lam_ref, lam_init)
        o_ref[...] = _diff_finish(acc_ref[0] / l_ref[0], acc_ref[1] / l_ref[1], lam, g_ref[...], lam_init, eps)


def diff_sample(cfg, aq, dkv, cache, page_table, lam_vec, subln_g, layer_i, lam_init):
    dbsz = aq.shape[0]
    n_pages = page_table.shape[1]
    ah, W = cfg.a_heads, LANES
    out = pl.pallas_call(
        functools.partial(_diff_sample_kernel, d_qk=cfg.a_qk, lam_init=lam_init, eps=cfg.rms_eps, n_pages=n_pages),
        grid_spec=pltpu.PrefetchScalarGridSpec(
            num_scalar_prefetch=1,
            grid=(dbsz, n_pages),
            in_specs=[pl.BlockSpec((None, ah, W), lambda b, j, pt: (b, 0, 0)),
                      pl.BlockSpec((None, 2, ah, W), lambda b, j, pt: (b, 0, 0, 0)),
                      pl.BlockSpec((None, None, cfg.page, 2, ah, W),
                                   lambda b, j, pt: (layer_i, pt[b, j], 0, 0, 0, 0)),
                      pl.BlockSpec((None, 4, cfg.a_qk), lambda b, j, pt: (layer_i, 0, 0)),
                      pl.BlockSpec((None, 1, W), lambda b, j, pt: (layer_i, 0, 0))],
            out_specs=pl.BlockSpec((None, ah, W), lambda b, j, pt: (b, 0, 0)),
            scratch_shapes=[pltpu.VMEM((2, ah, 1), F32), pltpu.VMEM((2, ah, 1), F32), pltpu.VMEM((2, ah, W), F32)]),
        out_shape=jax.ShapeDtypeStruct((dbsz, ah, W), F32),
        compiler_params=_cp(("parallel", "arbitrary")),
        name="diff_sample",
    )(page_table, aq.reshape(dbsz, ah, W), dkv.reshape(dbsz, 2, ah, W), cache, lam_vec, subln_g.reshape(-1, 1, W))
    return out.reshape(dbsz, ah * W)


def _suffix_sum(x):
    n = x.shape[0]
    sh = 1
    while sh < n:
        x = x + jnp.concatenate([x[sh:], jnp.zeros((sh,) + x.shape[1:], x.dtype)], axis=0)
        sh *= 2
    return x


def _sb_sample_kernel(pt_ref, q_ref, c0_ref, a0_ref, page_ref, o_ref, cout_ref, c_ref, acc_ref, *, n_pages):
    del pt_ref
    j = pl.program_id(1)

    @pl.when(j == 0)
    def _():
        c_ref[...] = c0_ref[...]
        acc_ref[...] = a0_ref[...]

    q = q_ref[...] * (LANES ** -0.5)
    z = jnp.sum(page_ref[:, 0] * q[None], axis=-1, keepdims=True)
    sp = _softplus(z)
    inc = _suffix_sum(-sp)
    between = inc + sp + c_ref[...][None]
    w = jnp.exp(z - sp + between)
    acc_ref[...] += jnp.sum(w * page_ref[:, 1], axis=0)
    c_ref[...] += inc[0]

    @pl.when(j == n_pages - 1)
    def _():
        o_ref[...] = acc_ref[...]
        cout_ref[...] = c_ref[...]


SB_TAIL_PAGES = 4


def _sb_sample_pages(cfg, q3, cache, page_table, layer_i, first_page, n_walk, c0, a0):
    dbsz, ch, W = q3.shape
    vec = pl.BlockSpec((None, ch, W), lambda b, j, pt: (b, 0, 0))
    one = pl.BlockSpec((None, ch, 1), lambda b, j, pt: (b, 0, 0))
    return pl.pallas_call(
        functools.partial(_sb_sample_kernel, n_pages=n_walk),
        grid_spec=pltpu.PrefetchScalarGridSpec(
            num_scalar_prefetch=1,
            grid=(dbsz, n_walk),
            in_specs=[vec, one, vec,
                      pl.BlockSpec((None, None, cfg.page, 2, ch, W),
                                   lambda b, j, pt: (layer_i, pt[b, first_page - j], 0, 0, 0, 0))],
            out_specs=[vec, one],
            scratch_shapes=[pltpu.VMEM((ch, 1), F32), pltpu.VMEM((ch, W), F32)]),
        out_shape=[jax.ShapeDtypeStruct((dbsz, ch, W), F32), jax.ShapeDtypeStruct((dbsz, ch, 1), F32)],
        compiler_params=_cp(("parallel", "arbitrary")),
        name="sb_sample",
    )(page_table, q3, c0, a0, cache)


def sb_sample(cfg, q, cache, page_table, layer_i):
    dbsz = q.shape[0]
    n_pages = page_table.shape[1]
    ch, W = cfg.c_heads, LANES
    q3 = q.reshape(dbsz, ch, W)
    tail = min(SB_TAIL_PAGES, n_pages)
    out, carry = _sb_sample_pages(cfg, q3, cache, page_table, layer_i, n_pages - 1, tail,
                                  jnp.zeros((dbsz, ch, 1), F32), jnp.zeros((dbsz, ch, W), F32))
    if n_pages > tail:
        out = lax.cond(jnp.max(carry) > SB_EXIT,
                       lambda: _sb_sample_pages(cfg, q3, cache, page_table, layer_i, n_pages - 1 - tail,
                                                n_pages - tail, carry, out)[0],
                       lambda: out)
    return out.reshape(dbsz, ch * W)


def _nsa_cmp_sample_kernel(pt_ref, q_ref, page_ref, pe_ref, w_ref, o_ref, sel_ref, pool_ref,
                           *, cfg, n_pages, nb):
    del pt_ref
    j = pl.program_id(1)
    blk, g, W = cfg.cmp_block, cfg.b_kv, LANES
    rep = cfg.b_heads // g
    per_page = cfg.page // blk
    x = page_ref[...]
    for t in range(per_page):
        s = jnp.sum(x[t * blk:(t + 1) * blk], axis=0) * (1.0 / blk)
        for sl in range(2):
            for gg in range(g):
                pool_ref[sl, gg, pl.ds(j * per_page + t, 1), :] = s[sl, gg:gg + 1, :]

    @pl.when(j == n_pages - 1)
    def _():
        scale = W ** -0.5
        bidx = lax.broadcasted_iota(jnp.int32, (1, nb), 1)
        lane = lax.broadcasted_iota(jnp.int32, (1, W), 1)
        for gg in range(g):
            kv = []
            for sl in range(2):
                pe = jnp.mean(pe_ref[sl], axis=0, keepdims=True)
                kv.append(jnp.dot((pool_ref[sl, gg] + pe).astype(BF16), w_ref[sl].astype(BF16),
                                  preferred_element_type=F32))
            q = q_ref[gg * rep:(gg + 1) * rep, :]
            sc = _dot_t(q.astype(BF16), kv[0].astype(BF16)) * scale
            e = jnp.exp(sc - jnp.max(sc, axis=-1, keepdims=True))
            p = e / jnp.sum(e, axis=-1, keepdims=True)
            o_ref[gg * rep:(gg + 1) * rep, :] = jnp.dot(p.astype(BF16), kv[1].astype(BF16),
                                                        preferred_element_type=F32)
            imp = jnp.sum(p, axis=0, keepdims=True)
            forced = (bidx == 0) | (bidx > nb - cfg.n_local)
            work = jnp.where(forced, cfg.forced, imp)
            picks = jnp.zeros((1, W), jnp.int32)
            for t in range(cfg.top_n - 1):
                mx = jnp.max(work, axis=-1, keepdims=True)
                first = jnp.min(jnp.where(work == mx, bidx, nb), axis=-1, keepdims=True)
                picks = jnp.where(lane == t, first, picks)
                work = jnp.where(bidx == first, -jnp.inf, work)
            sel_ref[gg:gg + 1, :] = picks


def nsa_cmp_sample(cfg, bq_raw, cache, page_table, cmp_pe, cmp_w, layer_i):
    dbsz = bq_raw.shape[0]
    n_pages = page_table.shape[1]
    g, W, blk = cfg.b_kv, LANES, cfg.cmp_block
    nb = n_pages * cfg.page // blk
    assert nb >= cfg.top_n and cfg.page % blk == 0
    return pl.pallas_call(
        functools.partial(_nsa_cmp_sample_kernel, cfg=cfg, n_pages=n_pages, nb=nb),
        grid_spec=pltpu.PrefetchScalarGridSpec(
            num_scalar_prefetch=1,
            grid=(dbsz, n_pages),
            in_specs=[pl.BlockSpec((None, cfg.b_heads, W), lambda b, j, pt: (b, 0, 0)),
                      pl.BlockSpec((None, None, cfg.page, 2, g, W),
                                   lambda b, j, pt: (layer_i, pt[b, j], 0, 0, 0, 0)),
                      pl.BlockSpec((None, 2, blk, W), lambda b, j, pt: (layer_i, 0, 0, 0)),
                      pl.BlockSpec((None, 2, W, W), lambda b, j, pt: (layer_i, 0, 0, 0))],
            out_specs=[pl.BlockSpec((None, cfg.b_heads, W), lambda b, j, pt: (b, 0, 0)),
                       pl.BlockSpec((None, g, W), lambda b, j, pt: (b, 0, 0))],
            scratch_shapes=[pltpu.VMEM((2, g, nb, W), F32)]),
        out_shape=[jax.ShapeDtypeStruct((dbsz, cfg.b_heads, W), F32),
                   jax.ShapeDtypeStruct((dbsz, g, W), jnp.int32)],
        compiler_params=_cp(("parallel", "arbitrary")),
        name="nsa_cmp_sample",
    )(page_table, bq_raw.reshape(dbsz, cfg.b_heads, W), cache, cmp_pe, cmp_w)


def _nsa_sel_sample_kernel(pt_ref, sel_ref, q_ref, new_ref, blk0_ref, blk1_ref, win_ref, wnew_ref,
                           osel_ref, owin_ref, m_ref, l_ref, acc_ref, *, cfg, n_steps, keep):
    del pt_ref, sel_ref
    s_idx = pl.program_id(1)
    W = LANES
    rep = cfg.b_heads // cfg.b_kv
    scale = W ** -0.5

    @pl.when(s_idx == 0)
    def _():
        for r in range(rep):
            sc = jnp.sum(new_ref[2] * q_ref[r], axis=-1, keepdims=True) * scale
            m_ref[r] = sc
            l_ref[r] = jnp.ones(sc.shape, F32)
            acc_ref[r] = new_ref[3]

    sub = lax.broadcasted_iota(jnp.int32, blk0_ref.shape[:1] + blk0_ref.shape[2:], 1)
    kk = jnp.where(sub == 0, blk0_ref[:, 0], blk1_ref[:, 0])
    vv = jnp.where(sub == 0, blk0_ref[:, 1], blk1_ref[:, 1])
    for r in range(rep):
        s = jnp.sum(kk * q_ref[r][None], axis=-1, keepdims=True) * scale
        m_old = m_ref[r]
        m_new = jnp.maximum(m_old, jnp.max(s, axis=0))
        a = jnp.exp(m_old - m_new)
        p = jnp.exp(s - m_new[None])
        l_ref[r] = a * l_ref[r] + jnp.sum(p, axis=0)
        acc_ref[r] = a * acc_ref[r] + jnp.sum(p * vv, axis=0)
        m_ref[r] = m_new

    @pl.when(s_idx == n_steps - 1)
    def _():
        kw, vw = win_ref[:, 0], win_ref[:, 1]
        rowi = lax.broadcasted_iota(jnp.int32, (keep, cfg.b_kv, 1), 0)
        valid = rowi > keep - cfg.window
        for r in range(rep):
            osel_ref[r] = acc_ref[r] / l_ref[r]
            q = q_ref[r]
            s = jnp.where(valid, jnp.sum(kw * q[None], axis=-1, keepdims=True) * scale, NEG)
            sn = jnp.sum(wnew_ref[0] * q, axis=-1, keepdims=True) * scale
            mx = jnp.maximum(jnp.max(s, axis=0), sn)
            p = jnp.exp(s - mx[None])
            pn = jnp.exp(sn - mx)
            owin_ref[r] = (jnp.sum(p * vw, axis=0) + pn * wnew_ref[1]) / (jnp.sum(p, axis=0) + pn)


def nsa_sel_sample(cfg, bq_rot, nkv, wkv, cache, win_state, page_table, sel_idx, layer_i):
    dbsz = bq_rot.shape[0]
    g, W, blk = cfg.b_kv, LANES, cfg.cmp_block
    rep = cfg.b_heads // g
    per_page = cfg.page // blk
    keep = win_state.shape[2]
    n_steps = cfg.top_n - 1
    q = jnp.transpose(bq_rot.reshape(dbsz, g, rep, W), (0, 2, 1, 3))

    def blk_spec(gg):
        def index(b, s, pt, sel):
            bid = sel[(b * g + gg) * W + s]
            return (layer_i, pt[b, bid // per_page], bid % per_page, 1, 0, 0)
        return pl.BlockSpec((None, None, blk, 2, g, W), index)

    assert g == 2
    out_spec = pl.BlockSpec((None, rep, g, W), lambda b, s, pt, sel: (b, 0, 0, 0))
    return pl.pallas_call(
        functools.partial(_nsa_sel_sample_kernel, cfg=cfg, n_steps=n_steps, keep=keep),
        grid_spec=pltpu.PrefetchScalarGridSpec(
            num_scalar_prefetch=2,
            grid=(dbsz, n_steps),
            in_specs=[pl.BlockSpec((None, rep, g, W), lambda b, s, pt, sel: (b, 0, 0, 0)),
                      pl.BlockSpec((None, 4, g, W), lambda b, s, pt, sel: (b, 0, 0, 0)),
                      blk_spec(0), blk_spec(1),
                      pl.BlockSpec((None, None, keep, 2, g, W), lambda b, s, pt, sel: (layer_i, b, 0, 0, 0, 0)),
                      pl.BlockSpec((None, 2, g, W), lambda b, s, pt, sel: (b, 0, 0, 0))],
            out_specs=[out_spec, out_spec],
            scratch_shapes=[pltpu.VMEM((rep, g, 1), F32), pltpu.VMEM((rep, g, 1), F32),
                            pltpu.VMEM((rep, g, W), F32)]),
        out_shape=[jax.ShapeDtypeStruct((dbsz, rep, g, W), F32)] * 2,
        compiler_params=_cp(("parallel", "arbitrary")),
        name="nsa_sel_sample",
    )(page_table, sel_idx.reshape(-1), q, nkv.reshape(dbsz, 4, g, W), cache, cache, win_state,
      wkv.reshape(dbsz, 2, g, W))


def _gather_pages(pool, page_table):
    rows = pool[page_table]
    return rows.reshape((page_table.shape[0], page_table.shape[1] * pool.shape[1]) + pool.shape[2:])


def _diff_sample_jnp(cfg, aq, all_k, all_v, lam, subln_g, lam_init):
    b = aq.shape[0]
    q = aq.reshape(b, cfg.a_heads, 2, cfg.a_qk)
    k = all_k.reshape(b, -1, cfg.a_heads, 2, cfg.a_qk)
    sc = jnp.einsum('bhmd,bkhmd->bhmk', q, k, precision=lax.Precision.HIGHEST) * cfg.a_qk ** -0.5
    p = jax.nn.softmax(sc, axis=-1)
    w = p[:, :, 0] - lam * p[:, :, 1]
    o = jnp.einsum('bhk,bkhd->bhd', w, all_v, precision=lax.Precision.HIGHEST)
    o = o * lax.rsqrt(jnp.mean(o * o, axis=-1, keepdims=True) + cfg.rms_eps) * subln_g * (1.0 - lam_init)
    return o.reshape(b, -1)


def _forward(cfg, x_prompt, x_sample, cache_diff_kv, cache_nsa_kv, state_nsa_win, cache_sb_kv, page_table,
             even_w_in, even_w_out, diff_lambda, diff_subln_g, nsa_cmp_pe, nsa_cmp_w,
             odd_w_in, odd_w_out, ln1_g, ln1_b, ln2_g, ln2_b,
             ffn_w_gu, ffn_w_down, moe_router, moe_w_gu, moe_w_down):
    bsz, n_t, d = x_prompt.shape
    dbsz = x_sample.shape[0]
    past_len = page_table.shape[1] * cfg.page
    W = LANES
    xp = x_prompt.reshape(bsz * n_t, d)
    xs = x_sample.reshape(dbsz, d)
    tabs_p = rope_tables(cfg, jnp.arange(n_t, dtype=jnp.int32))
    tabs_s = rope_tables(cfg, jnp.full((dbsz,), past_len, dtype=jnp.int32))
    outs = {k: [] for k in ("diff_p", "diff_s", "nsa_p", "nsa_s", "win_p", "win_s", "sb_p", "sb_s")}
    n_gate = 3 * cfg.b_heads
    for layer in range(cfg.depth):
        i = layer // 2
        if layer % 2 == 0:
            lam_init = 0.8 - 0.6 * math.exp(-0.3 * layer)
            w_gate = jnp.pad(even_w_in[i][:, cfg.even_main:], ((0, 0), (0, W - n_gate)))
            proj = matmul(xp, even_w_in, (i,), 0, cfg.even_main)
            gates = matmul(xp, w_gate)
            aq, dkv, bqr, bqt, nkv, wkv = even_split(cfg, proj, tabs_p, n_t)
            a_o = diff_prompt(cfg, aq, dkv, diff_lambda, diff_subln_g, i, lam_init, bsz, n_t)
            ckv = cmp_prep(cfg, nkv, nsa_cmp_pe, nsa_cmp_w, i)
            b_o = nsa_prompt(cfg, bqr, bqt, ckv, nkv, wkv, gates, bsz, n_t)
            mp = matmul(jnp.concatenate([a_o, b_o], axis=-1), even_w_out, (i,))
            keep = min(cfg.window, n_t)
            outs["diff_p"].append(dkv.reshape(bsz, n_t, 2, cfg.a_heads, W))
            outs["nsa_p"].append(nkv.reshape(bsz, n_t, 4, cfg.b_kv, W))
            outs["win_p"].append(wkv.reshape(bsz, n_t, 2, cfg.b_kv, W)[:, n_t - keep:])
            ms, dks, nks, wsn = _even_sample(cfg, xs, even_w_in, w_gate, even_w_out, i, tabs_s, lam_init,
                                             cache_diff_kv, cache_nsa_kv, state_nsa_win, page_table,
                                             diff_lambda, diff_subln_g, nsa_cmp_pe, nsa_cmp_w, past_len)
            outs["diff_s"].append(dks)
            outs["nsa_s"].append(nks)
            outs["win_s"].append(wsn)
        else:
            qkv = matmul(xp, odd_w_in, (i,))
            o = sb_prompt(cfg, qkv, bsz, n_t)
            mp = matmul(o, odd_w_out, (i,))
            outs["sb_p"].append(qkv[:, cfg.c_heads * W:].reshape(bsz, n_t, 2, cfg.c_heads, W))
            ms, sks = _odd_sample(cfg, xs, odd_w_in, odd_w_out, i, cache_sb_kv, page_table)
            outs["sb_s"].append(sks)
        xp = add_ln(cfg, xp, mp, ln1_g, ln1_b, layer)
        xs = add_ln(cfg, xs, ms, ln1_g, ln1_b, layer)
        if layer % 2 == 0:
            fp = matmul(swiglu_up(xp, ffn_w_gu, (i,), cfg.d_ff), ffn_w_down, (i,), tm=512)
            fs = matmul(swiglu_up(xs, ffn_w_gu, (i,), cfg.d_ff), ffn_w_down, (i,), tm=512)
            xp = add_ln(cfg, xp, fp, ln2_g, ln2_b, layer)
            xs = add_ln(cfg, xs, fs, ln2_g, ln2_b, layer)
        else:
            xp, xs = moe_layer(cfg, xp, xs, moe_router, moe_w_gu, moe_w_down, i, ln2_g, ln2_b, layer)
    st = {k: jnp.stack(v) for k, v in outs.items()}
    return (xp.reshape(bsz, n_t, d), xs.reshape(dbsz, 1, d), st["diff_p"], st["diff_s"], st["nsa_p"], st["nsa_s"],
            st["win_p"], st["win_s"], st["sb_p"], st["sb_s"])


def _even_sample(cfg, xs, even_w_in, w_gate, even_w_out, i, tabs_s, lam_init,
                 cache_diff_kv, cache_nsa_kv, state_nsa_win, page_table,
                 diff_lambda, diff_subln_g, nsa_cmp_pe, nsa_cmp_w, past_len):
    dbsz = xs.shape[0]
    W, g = LANES, cfg.b_kv
    rep = cfg.b_heads // g
    assert past_len % cfg.cmp_block == 0
    proj = matmul(xs, even_w_in, (i,), 0, cfg.even_main)
    gates = jax.nn.sigmoid(matmul(xs, w_gate)[:, :3 * cfg.b_heads]).reshape(dbsz, cfg.b_heads, 3)
    aq, dkv, bqr, bqt, nkv, wkv = even_split(cfg, proj, tabs_s, dbsz)
    a_o = diff_sample(cfg, aq, dkv, cache_diff_kv, page_table, diff_lambda, diff_subln_g, i, lam_init)
    o_cmp, sel_idx = nsa_cmp_sample(cfg, bqr, cache_nsa_kv, page_table, nsa_cmp_pe, nsa_cmp_w, i)
    o_sel, o_win = nsa_sel_sample(cfg, bqt, nkv, wkv, cache_nsa_kv, state_nsa_win, page_table, sel_idx, i)
    o_sel = jnp.transpose(o_sel, (0, 2, 1, 3)).reshape(dbsz, cfg.b_heads, W)
    o_win = jnp.transpose(o_win, (0, 2, 1, 3)).reshape(dbsz, cfg.b_heads, W)
    b_o = gates[..., 0:1] * o_cmp + gates[..., 1:2] * o_sel + gates[..., 2:3] * o_win
    ms = matmul(jnp.concatenate([a_o, b_o.reshape(dbsz, -1)], axis=-1), even_w_out, (i,))
    keep = state_nsa_win.shape[2]
    win = jnp.concatenate([state_nsa_win[i], wkv.reshape(dbsz, 1, 2, g, W)], axis=1)
    return (ms, dkv.reshape(dbsz, 1, 2, cfg.a_heads, W), nkv.reshape(dbsz, 1, 4, g, W), win[:, 1:keep + 1])


def _odd_sample(cfg, xs, odd_w_in, odd_w_out, i, cache_sb_kv, page_table):
    dbsz = xs.shape[0]
    ch, W = cfg.c_heads, LANES
    qkv = matmul(xs, odd_w_in, (i,))
    o = sb_sample(cfg, qkv[:, :ch * W], cache_sb_kv, page_table, i)
    return matmul(o, odd_w_out, (i,)), qkv[:, ch * W:].reshape(dbsz, 1, 2, ch, W)


def _even_sample_jnp(cfg, xs, even_w_in, w_gate, even_w_out, i, tabs_s, lam_init,
                     cache_diff_kv, cache_nsa_kv, state_nsa_win, page_table,
                     diff_lambda, diff_subln_g, nsa_cmp_pe, nsa_cmp_w, past_len):
    dbsz = xs.shape[0]
    W, g = LANES, cfg.b_kv
    rep = cfg.b_heads // g
    hp = lax.Precision.HIGHEST
    proj = matmul(xs, even_w_in, (i,), 0, cfg.even_main)
    gates = jax.nn.sigmoid(matmul(xs, w_gate)[:, :3 * cfg.b_heads]).reshape(dbsz, cfg.b_heads, 3)
    aq, dkv, bqr, bqt, nkv, wkv = even_split(cfg, proj, tabs_s, dbsz)
    new_diff = dkv.reshape(dbsz, 1, 2, cfg.a_heads, W)
    new_nsa = nkv.reshape(dbsz, 1, 4, g, W)
    new_win = wkv.reshape(dbsz, 1, 2, g, W)
    all_diff = jnp.concatenate([_gather_pages(cache_diff_kv[i], page_table), new_diff], axis=1)
    all_nsa = jnp.concatenate([_gather_pages(cache_nsa_kv[i], page_table), new_nsa], axis=1)
    win = jnp.concatenate([state_nsa_win[i], new_win], axis=1)
    lv = diff_lambda[i]
    lam = jnp.exp(jnp.sum(lv[0] * lv[1])) - jnp.exp(jnp.sum(lv[2] * lv[3])) + lam_init
    a_o = _diff_sample_jnp(cfg, aq, all_diff[:, :, 0], all_diff[:, :, 1], lam, diff_subln_g[i], lam_init)
    blk = cfg.cmp_block
    seq_len = past_len + 1
    nb_c = seq_len // blk
    blocks = all_nsa[:, :nb_c * blk, 0:2].reshape(dbsz, nb_c, blk, 2, g, W)
    summ = jnp.mean(blocks + jnp.transpose(nsa_cmp_pe[i], (1, 0, 2))[:, :, None, :], axis=2)
    summ = jnp.einsum('bnjgd,jde->bnjge', summ, nsa_cmp_w[i], precision=hp)
    k_cmp, v_cmp = summ[:, :, 0], summ[:, :, 1]
    scale = W ** -0.5
    qg = bqr.reshape(dbsz, g, rep, W)
    sc = jnp.einsum('bgrd,bngd->bgrn', qg, k_cmp, precision=hp) * scale
    p_cmp = jax.nn.softmax(sc, axis=-1)
    o_cmp = jnp.einsum('bgrn,bngd->bgrd', p_cmp, v_cmp, precision=hp).reshape(dbsz, cfg.b_heads, W)
    nb_s = -(-seq_len // blk)
    imp = jnp.pad(jnp.sum(p_cmp, axis=2), ((0, 0), (0, 0), (0, nb_s - nb_c)))
    bidx = jnp.arange(nb_s, dtype=jnp.int32)
    cur = past_len // blk
    forced = (bidx == 0) | ((bidx <= cur) & (bidx > cur - cfg.n_local))
    score = jnp.where(bidx > cur, -jnp.inf, jnp.where(forced, cfg.forced, imp))
    _, sel_idx = lax.top_k(score, min(cfg.top_n, nb_s))
    sel = jnp.pad(all_nsa[:, :, 2:4], ((0, 0), (0, nb_s * blk - seq_len), (0, 0), (0, 0), (0, 0)))
    sel_blocks = jnp.transpose(sel.reshape(dbsz, nb_s, blk, 2, g, W), (0, 4, 1, 2, 3, 5))
    gath = sel_blocks[jnp.arange(dbsz)[:, None, None], jnp.arange(g)[None, :, None], sel_idx]
    gath = gath.reshape(dbsz, g, -1, 2, W)
    kpos = (sel_idx[..., None] * blk + jnp.arange(blk, dtype=jnp.int32)).reshape(dbsz, g, -1)
    qt = bqt.reshape(dbsz, g, rep, W)
    ssc = jnp.einsum('bgrd,bgkd->bgrk', qt, gath[..., 0, :], precision=hp) * scale
    ps = jax.nn.softmax(jnp.where((kpos <= past_len)[:, :, None], ssc, -jnp.inf), axis=-1)
    o_sel = jnp.einsum('bgrk,bgkd->bgrd', ps, gath[..., 1, :], precision=hp).reshape(dbsz, cfg.b_heads, W)
    keep = state_nsa_win.shape[2]
    kp = past_len - keep + jnp.arange(keep + 1, dtype=jnp.int32)
    wmask = kp > past_len - cfg.window
    wsc = jnp.einsum('bgrd,bkgd->bgrk', qt, win[:, :, 0], precision=hp) * scale
    pw = jax.nn.softmax(jnp.where(wmask, wsc, -jnp.inf), axis=-1)
    o_win = jnp.einsum('bgrk,bkgd->bgrd', pw, win[:, :, 1], precision=hp).reshape(dbsz, cfg.b_heads, W)
    b_o = gates[..., 0:1] * o_cmp + gates[..., 1:2] * o_sel + gates[..., 2:3] * o_win
    mixed = jnp.concatenate([a_o, b_o.reshape(dbsz, -1)], axis=-1)
    ms = matmul(mixed, even_w_out, (i,))
    return ms, new_diff, new_nsa, win[:, win.shape[1] - keep:]


def _odd_sample_jnp(cfg, xs, odd_w_in, odd_w_out, i, cache_sb_kv, page_table):
    dbsz = xs.shape[0]
    ch, W = cfg.c_heads, LANES
    hp = lax.Precision.HIGHEST
    qkv = matmul(xs, odd_w_in, (i,))
    q = qkv[:, :ch * W].reshape(dbsz, ch, W)
    new_kv = qkv[:, ch * W:].reshape(dbsz, 1, 2, ch, W)
    past = _gather_pages(cache_sb_kv[i], page_table)
    z = jnp.einsum('bhd,bkhd->bhk', q, past[:, :, 0], precision=hp) * W ** -0.5
    log1m = -jax.nn.softplus(z)
    between = lax.cumsum(log1m, axis=2, reverse=True) - log1m
    w = jnp.exp(jax.nn.log_sigmoid(z) + between)
    o = jnp.einsum('bhk,bkhd->bhd', w, past[:, :, 1], precision=hp).reshape(dbsz, ch * W)
    return matmul(o, odd_w_out, (i,)), new_kv


def kernel(x_prompt, x_sample, cache_diff_kv, cache_nsa_kv, state_nsa_win, cache_sb_kv, page_table, even_w_in, even_w_out, diff_lambda, diff_subln_g, nsa_cmp_pe, nsa_cmp_w, odd_w_in, odd_w_out, ln1_g, ln1_b, ln2_g, ln2_b, ffn_w_gu, ffn_w_down, moe_router, moe_w_gu, moe_w_down):
    return _forward(Cfg(), x_prompt, x_sample, cache_diff_kv, cache_nsa_kv, state_nsa_win, cache_sb_kv, page_table,
                    even_w_in, even_w_out, diff_lambda, diff_subln_g, nsa_cmp_pe, nsa_cmp_w,
                    odd_w_in, odd_w_out, ln1_g, ln1_b, ln2_g, ln2_b,
                    ffn_w_gu, ffn_w_down, moe_router, moe_w_gu, moe_w_down)
```

```python
import functools
import math
from typing import NamedTuple

import jax
import jax.numpy as jnp
from jax import lax
from jax.experimental import pallas as pl
from jax.experimental.pallas import tpu as pltpu

F32 = jnp.float32
BF16 = jnp.bfloat16

LANES = 128
VMEM_LIMIT = 56 * 1024 * 1024
NEG = -1.0e30
SB_EXIT = -104.0


class Cfg(NamedTuple):
    d_model: int = 2048
    depth: int = 4
    page: int = 128
    a_heads: int = 8
    a_qk: int = 64
    b_heads: int = 8
    b_kv: int = 2
    cmp_block: int = 64
    top_n: int = 16
    n_local: int = 2
    window: int = 512
    forced: float = 1.0e4
    c_heads: int = 16
    d_ff: int = 5632
    n_exp: int = 8
    top_k: int = 2
    theta: float = 10000.0
    q_block: int = 128
    ln_eps: float = 1e-5
    rms_eps: float = 1e-5

    @property
    def alpha(self):
        return (2 * self.depth) ** 0.25

    @property
    def even_main(self):
        return 4 * self.a_heads * LANES + 6 * self.b_kv * LANES


def _cp(sem, vmem=VMEM_LIMIT):
    return pltpu.CompilerParams(dimension_semantics=sem, vmem_limit_bytes=vmem)


def _pick(n, pref):
    if n <= pref:
        return n
    t = pref
    while n % t:
        t //= 2
    return t


def _mm_kernel(x_ref, w_ref, o_ref, *scratch, nk):
    part = jnp.dot(x_ref[...].astype(BF16), w_ref[...].astype(BF16), preferred_element_type=F32)
    if nk == 1:
        o_ref[...] = part.astype(o_ref.dtype)
        return
    acc_ref, = scratch
    k = pl.program_id(2)

    @pl.when(k == 0)
    def _():
        acc_ref[...] = part

    @pl.when(k > 0)
    def _():
        acc_ref[...] += part

    @pl.when(k == nk - 1)
    def _():
        o_ref[...] = acc_ref[...].astype(o_ref.dtype)


def matmul(x, w, lead=(), col0=0, n=None, tm=2048, tn=256, tk=None, out_dtype=F32, w_outer=False):
    m, kdim = x.shape
    n = w.shape[-1] - col0 if n is None else n
    tm = _pick(m, tm)
    tn = _pick(n, tn)
    tk = kdim if tk is None else _pick(kdim, tk)
    assert col0 % tn == 0 and m % tm == 0 and n % tn == 0 and kdim % tk == 0
    nk = kdim // tk
    cb0 = col0 // tn
    lead = tuple(lead)

    def ij(a, b):
        return (b, a) if w_outer else (a, b)

    grid = ij(m // tm, n // tn) + (nk,)
    return pl.pallas_call(
        functools.partial(_mm_kernel, nk=nk),
        grid=grid,
        in_specs=[pl.BlockSpec((tm, tk), lambda a, b, k: (ij(a, b)[0], k)),
                  pl.BlockSpec((None,) * len(lead) + (tk, tn), lambda a, b, k: lead + (k, cb0 + ij(a, b)[1]))],
        out_specs=pl.BlockSpec((tm, tn), lambda a, b, k: ij(a, b)),
        out_shape=jax.ShapeDtypeStruct((m, n), out_dtype),
        scratch_shapes=[pltpu.VMEM((tm, tn), F32)] if nk > 1 else [],
        compiler_params=_cp(("parallel", "parallel", "arbitrary")),
        name="matmul",
    )(x, w)


def _mm2_kernel(xa_ref, xb_ref, wa_ref, wb_ref, o_ref):
    o_ref[...] = (jnp.dot(xa_ref[...].astype(BF16), wa_ref[...].astype(BF16), preferred_element_type=F32)
                  + jnp.dot(xb_ref[...].astype(BF16), wb_ref[...].astype(BF16), preferred_element_type=F32))


def matmul_cat(xa, xb, w, lead, tm=2048, tn=256):
    m, ka = xa.shape
    assert xb.shape == xa.shape and w.shape[-2] == 2 * ka
    n = w.shape[-1]
    tm, tn = _pick(m, tm), _pick(n, tn)
    lead = tuple(lead)
    wblk = (None,) * len(lead) + (ka, tn)
    return pl.pallas_call(
        _mm2_kernel,
        grid=(m // tm, n // tn),
        in_specs=[pl.BlockSpec((tm, ka), lambda i, j: (i, 0)), pl.BlockSpec((tm, ka), lambda i, j: (i, 0)),
                  pl.BlockSpec(wblk, lambda i, j: lead + (0, j)), pl.BlockSpec(wblk, lambda i, j: lead + (1, j))],
        out_specs=pl.BlockSpec((tm, tn), lambda i, j: (i, j)),
        out_shape=jax.ShapeDtypeStruct((m, n), F32),
        compiler_params=_cp(("parallel", "parallel")),
        name="matmul_cat",
    )(xa, xb, w, w)


def _swiglu_kernel(x_ref, wg_ref, wu_ref, o_ref):
    x = x_ref[...].astype(BF16)
    g = jnp.dot(x, wg_ref[...].astype(BF16), preferred_element_type=F32)
    u = jnp.dot(x, wu_ref[...].astype(BF16), preferred_element_type=F32)
    o_ref[...] = (g * jax.nn.sigmoid(g) * u).astype(o_ref.dtype)


def swiglu_up(x, w_gu, lead, d_ff, tm=1024, tn=512):
    m, kdim = x.shape
    tm = _pick(m, tm)
    tn = _pick(d_ff, tn)
    nb = d_ff // tn
    nlead = len(lead)
    wblk = (None,) * nlead + (kdim, tn)
    return pl.pallas_call(
        _swiglu_kernel,
        grid=(m // tm, nb),
        in_specs=[pl.BlockSpec((tm, kdim), lambda i, j: (i, 0)),
                  pl.BlockSpec(wblk, lambda i, j: tuple(lead) + (0, j)),
                  pl.BlockSpec(wblk, lambda i, j: tuple(lead) + (0, nb + j))],
        out_specs=pl.BlockSpec((tm, tn), lambda i, j: (i, j)),
        out_shape=jax.ShapeDtypeStruct((m, d_ff), BF16),
        compiler_params=_cp(("parallel", "parallel")),
        name="swiglu_up",
    )(x, w_gu, w_gu)


def _ln_kernel(x_ref, f_ref, g_ref, b_ref, o_ref, *, alpha, eps):
    y = alpha * x_ref[...] + f_ref[...].astype(F32)
    mu = jnp.mean(y, axis=-1, keepdims=True)
    d = y - mu
    var = jnp.mean(d * d, axis=-1, keepdims=True)
    o_ref[...] = d * lax.rsqrt(var + eps) * g_ref[...] + b_ref[...]


def add_ln(cfg, x, f, g, b, layer, tm=256):
    m, d = x.shape
    tm = _pick(m, tm)
    return pl.pallas_call(
        functools.partial(_ln_kernel, alpha=cfg.alpha, eps=cfg.ln_eps),
        grid=(m // tm,),
        in_specs=[pl.BlockSpec((tm, d), lambda i: (i, 0)),
                  pl.BlockSpec((tm, d), lambda i: (i, 0)),
                  pl.BlockSpec((None, 1, d), lambda i: (layer, 0, 0)),
                  pl.BlockSpec((None, 1, d), lambda i: (layer, 0, 0))],
        out_specs=pl.BlockSpec((tm, d), lambda i: (i, 0)),
        out_shape=jax.ShapeDtypeStruct((m, d), F32),
        compiler_params=_cp(("parallel",)),
        name="add_ln",
    )(x, f, g.reshape(-1, 1, d), b.reshape(-1, 1, d))


def rope_tables(cfg, pos):
    out = []
    for d in (cfg.a_qk, LANES):
        half = d // 2
        inv = cfg.theta ** (-jnp.arange(half, dtype=F32) / half)
        ang = pos.astype(F32)[:, None] * inv[None, :]
        c, s = jnp.cos(ang), jnp.sin(ang)
        reps = LANES // d
        out.append(jnp.tile(jnp.concatenate([c, c], axis=-1), (1, reps)))
        out.append(jnp.tile(jnp.concatenate([-s, s], axis=-1), (1, reps)))
    return out


def _rope(x, c, s, d):
    if d == LANES:
        partner = pltpu.roll(x, LANES // 2, 1)
    else:
        lane = lax.broadcasted_iota(jnp.int32, x.shape, 1)
        lo = (lane % d) < (d // 2)
        partner = jnp.where(lo, pltpu.roll(x, LANES - d // 2, 1), pltpu.roll(x, d // 2, 1))
    return x * c + partner * s


def _split_kernel(p_ref, c64_ref, s64_ref, c128_ref, s128_ref,
                  aq_ref, dkv_ref, bqr_ref, bqt_ref, nkv_ref, win_ref, *, cfg):
    c64, s64, c128, s128 = c64_ref[...], s64_ref[...], c128_ref[...], s128_ref[...]
    ah, g = cfg.a_heads, cfg.b_kv
    W = LANES

    def col(i):
        return p_ref[:, i * W:(i + 1) * W]

    for h in range(ah):
        aq_ref[:, h * W:(h + 1) * W] = _rope(col(h), c64, s64, cfg.a_qk)
        dkv_ref[:, h * W:(h + 1) * W] = _rope(col(ah + h), c64, s64, cfg.a_qk)
        dkv_ref[:, (ah + h) * W:(ah + h + 1) * W] = col(2 * ah + h)
    for h in range(cfg.b_heads):
        x = col(3 * ah + h)
        bqr_ref[:, h * W:(h + 1) * W] = x
        bqt_ref[:, h * W:(h + 1) * W] = _rope(x, c128, s128, W)
    base = 3 * ah + cfg.b_heads
    for s in range(4):
        for j in range(g):
            x = col(base + s * g + j)
            nkv_ref[:, (s * g + j) * W:(s * g + j + 1) * W] = _rope(x, c128, s128, W) if s == 2 else x
    base += 4 * g
    for s in range(2):
        for j in range(g):
            x = col(base + s * g + j)
            win_ref[:, (s * g + j) * W:(s * g + j + 1) * W] = _rope(x, c128, s128, W) if s == 0 else x


def even_split(cfg, proj, tabs, n_pos):
    m = proj.shape[0]
    tm = _pick(n_pos, 256)
    npb = n_pos // tm
    ah, g, W = cfg.a_heads, cfg.b_kv, LANES
    widths = (ah * W, 2 * ah * W, cfg.b_heads * W, cfg.b_heads * W, 4 * g * W, 2 * g * W)
    tab_spec = pl.BlockSpec((tm, W), lambda i: (i % npb, 0))
    return pl.pallas_call(
        functools.partial(_split_kernel, cfg=cfg),
        grid=(m // tm,),
        in_specs=[pl.BlockSpec((tm, proj.shape[1]), lambda i: (i, 0))] + [tab_spec] * 4,
        out_specs=[pl.BlockSpec((tm, w), lambda i: (i, 0)) for w in widths],
        out_shape=[jax.ShapeDtypeStruct((m, w), F32) for w in widths],
        compiler_params=_cp(("parallel",)),
        name="even_split",
    )(proj, *tabs)


def _softmax_reset(m_ref, l_ref, acc_ref):
    m_ref[...] = jnp.full(m_ref.shape, NEG, F32)
    l_ref[...] = jnp.zeros(l_ref.shape, F32)
    acc_ref[...] = jnp.zeros(acc_ref.shape, F32)


def _softmax_step(s, m_ref, l_ref, acc_ref, v):
    chunks = [s[:, c * LANES:(c + 1) * LANES] for c in range(s.shape[1] // LANES)]
    cmax = functools.reduce(jnp.maximum, chunks)
    m_old = m_ref[...]
    m_new = jnp.maximum(m_old, jnp.max(cmax, axis=-1, keepdims=True))
    a = jnp.exp(m_old - m_new)
    ps = [jnp.exp(c - m_new) for c in chunks]
    l_ref[...] = a * l_ref[...] + functools.reduce(jnp.add, ps)
    p = jnp.concatenate(ps, axis=1).astype(BF16)
    acc_ref[...] = a * acc_ref[...] + jnp.dot(p, v, preferred_element_type=F32)
    m_ref[...] = m_new


def _softmax_result(l_ref, acc_ref):
    return acc_ref[...] / jnp.sum(l_ref[...], axis=-1, keepdims=True)


def _dot_t(a, b):
    return lax.dot_general(a, b, (((1,), (1,)), ((), ())), preferred_element_type=F32)


def _diff_lambda(lam_ref, lam_init):
    lv = lam_ref[...]
    a = jnp.sum(lv[0:1] * lv[1:2], axis=-1, keepdims=True)
    b = jnp.sum(lv[2:3] * lv[3:4], axis=-1, keepdims=True)
    return jnp.exp(a) - jnp.exp(b) + lam_init


def _diff_finish(o1, o2, lam, g, lam_init, eps):
    o = o1 - lam * o2
    o = o * lax.rsqrt(jnp.mean(o * o, axis=-1, keepdims=True) + eps)
    return o * g * (1.0 - lam_init)


def _diff_prompt_kernel(q_ref, k_ref, v_ref, lam_ref, g_ref, o_ref, m_ref, l_ref, acc_ref,
                        *, tq, tk, d_qk, lam_init, eps):
    i = pl.program_id(2)
    q = q_ref[...] * (d_qk ** -0.5)
    lane = lax.broadcasted_iota(jnp.int32, q.shape, 1)
    q2 = jnp.concatenate([jnp.where(lane < d_qk, q, 0.0), jnp.where(lane >= d_qk, q, 0.0)], axis=0).astype(BF16)
    _softmax_reset(m_ref, l_ref, acc_ref)
    n_full = (i * tq) // tk

    def tile(j, masked):
        k = k_ref[pl.ds(pl.multiple_of(j * tk, tk), tk), :].astype(BF16)
        v = v_ref[pl.ds(pl.multiple_of(j * tk, tk), tk), :].astype(BF16)
        s = _dot_t(q2, k)
        if masked:
            qpos = i * tq + lax.broadcasted_iota(jnp.int32, s.shape, 0) % tq
            kpos = j * tk + lax.broadcasted_iota(jnp.int32, s.shape, 1)
            s = jnp.where(kpos <= qpos, s, NEG)
        _softmax_step(s, m_ref, l_ref, acc_ref, v)

    def body(j, c):
        tile(j, False)
        return c

    lax.fori_loop(0, n_full, body, 0)
    tile(n_full, True)
    o = _softmax_result(l_ref, acc_ref)
    lam = _diff_lambda(lam_ref, lam_init)
    o_ref[...] = _diff_finish(o[:tq], o[tq:], lam, g_ref[...], lam_init, eps)


def diff_prompt(cfg, aq, dkv, lam_vec, subln_g, layer_i, lam_init, bsz, n_t, tq=256, tk=512):
    tq = _pick(n_t, tq)
    tk = _pick(n_t, tk)
    assert tk % tq == 0
    nq = n_t // tq
    ah, W = cfg.a_heads, LANES
    return pl.pallas_call(
        functools.partial(_diff_prompt_kernel, tq=tq, tk=tk, d_qk=cfg.a_qk, lam_init=lam_init, eps=cfg.rms_eps),
        grid=(bsz, ah, nq),
        in_specs=[pl.BlockSpec((tq, W), lambda b, h, i: (b * nq + i, h)),
                  pl.BlockSpec((n_t, W), lambda b, h, i: (b, h)),
                  pl.BlockSpec((n_t, W), lambda b, h, i: (b, ah + h)),
                  pl.BlockSpec((None, 4, cfg.a_qk), lambda b, h, i: (layer_i, 0, 0)),
                  pl.BlockSpec((None, 1, W), lambda b, h, i: (layer_i, 0, 0))],
        out_specs=pl.BlockSpec((tq, W), lambda b, h, i: (b * nq + i, h)),
        out_shape=jax.ShapeDtypeStruct((bsz * n_t, ah * W), F32),
        scratch_shapes=[pltpu.VMEM((2 * tq, W), F32)] * 3,
        compiler_params=_cp(("parallel", "parallel", "arbitrary")),
        name="diff_prompt",
    )(aq, dkv, dkv, lam_vec, subln_g.reshape(-1, 1, W))


def _cmp_prep_kernel(kv_ref, pe_ref, w_ref, o_ref, *, blk, g):
    rows = kv_ref[...]
    tb = rows.shape[0] // blk
    r = lax.broadcasted_iota(jnp.int32, (tb, rows.shape[0]), 0)
    c = lax.broadcasted_iota(jnp.int32, (tb, rows.shape[0]), 1)
    pool = jnp.where(c // blk == r, 1.0 / blk, 0.0).astype(F32)
    mean = jnp.dot(pool, rows, preferred_element_type=F32, precision=lax.Precision.HIGHEST)
    for j in range(2):
        pe = jnp.mean(pe_ref[j], axis=0, keepdims=True)
        for gg in range(g):
            sl = slice((j * g + gg) * LANES, (j * g + gg + 1) * LANES)
            o_ref[:, sl] = jnp.dot((mean[:, sl] + pe).astype(BF16), w_ref[j].astype(BF16),
                                   preferred_element_type=F32)


def cmp_prep(cfg, nsa_kv, cmp_pe, cmp_w, layer_i):
    m = nsa_kv.shape[0]
    blk, g, W = cfg.cmp_block, cfg.b_kv, LANES
    nb = m // blk
    tb = _pick(nb, 8)
    return pl.pallas_call(
        functools.partial(_cmp_prep_kernel, blk=blk, g=g),
        grid=(nb // tb,),
        in_specs=[pl.BlockSpec((tb * blk, 2 * g * W), lambda i: (i, 0)),
                  pl.BlockSpec((None, 2, blk, W), lambda i: (layer_i, 0, 0, 0)),
                  pl.BlockSpec((None, 2, W, W), lambda i: (layer_i, 0, 0, 0))],
        out_specs=pl.BlockSpec((tb, 2 * g * W), lambda i: (i, 0)),
        out_shape=jax.ShapeDtypeStruct((nb, 2 * g * W), F32),
        compiler_params=_cp(("parallel",)),
        name="cmp_prep",
    )(nsa_kv, cmp_pe, cmp_w)


def _top_n_mask(score, n_sel):
    nb = score.shape[-1]
    idx = lax.broadcasted_iota(jnp.int32, score.shape, 1)
    sel = jnp.zeros(score.shape, F32)
    work = score
    for _ in range(n_sel):
        mx = jnp.max(work, axis=-1, keepdims=True)
        first = jnp.min(jnp.where(work == mx, idx, nb), axis=-1, keepdims=True)
        hit = idx == first
        sel = jnp.where(hit, 1.0, sel)
        work = jnp.where(hit, -jnp.inf, work)
    return sel


def _nsa_prompt_kernel(qr_ref, qt_ref, ckv_ref, ks_ref, vs_ref, kw_ref, vw_ref, gate_ref, o_ref,
                       m_ref, l_ref, acc_ref, *, cfg, tq, tk, wspan, rep, nb):
    gidx = pl.program_id(1)
    i = pl.program_id(2)
    W = LANES
    blk = cfg.cmp_block
    scale = W ** -0.5
    rows = rep * tq
    qpos = i * tq + lax.broadcasted_iota(jnp.int32, (tq, 1), 0)

    kc = ckv_ref[:, pl.ds(pl.multiple_of(gidx * W, W), W)]
    vc = ckv_ref[:, pl.ds(pl.multiple_of((cfg.b_kv + gidx) * W, W), W)]
    bidx = lax.broadcasted_iota(jnp.int32, (tq, nb), 1)
    cvalid = (bidx + 1) * blk <= qpos + 1
    imp = jnp.zeros((tq, nb), F32)
    o_cmp = []
    for r in range(rep):
        q = qr_ref[:, r * W:(r + 1) * W]
        sc = _dot_t(q.astype(BF16), kc.astype(BF16)) * scale
        sc = jnp.where(cvalid, sc, -jnp.inf)
        mx = jnp.max(sc, axis=-1, keepdims=True)
        mx = jnp.where(mx > -jnp.inf, mx, 0.0)
        e = jnp.exp(sc - mx)
        p = e / jnp.maximum(jnp.sum(e, axis=-1, keepdims=True), 1e-30)
        imp = imp + p
        o_cmp.append(jnp.dot(p.astype(BF16), vc.astype(BF16), preferred_element_type=F32))
    cur = qpos // blk
    forced = (bidx == 0) | ((bidx <= cur) & (bidx > cur - cfg.n_local))
    score = jnp.where(bidx > cur, -jnp.inf, jnp.where(forced, cfg.forced, imp))
    sel = _top_n_mask(score, min(cfg.top_n, nb)).astype(BF16)

    q4 = jnp.concatenate([qt_ref[:, r * W:(r + 1) * W] for r in range(rep)], axis=0)
    q4 = (q4 * scale).astype(BF16)
    bpt = tk // blk
    n_full = (i * tq) // tk

    def sel_tile(j, diag):
        k = ks_ref[pl.ds(pl.multiple_of(j * tk, tk), tk), :].astype(BF16)
        v = vs_ref[pl.ds(pl.multiple_of(j * tk, tk), tk), :].astype(BF16)
        er = lax.broadcasted_iota(jnp.int32, (nb, tk), 0)
        ec = lax.broadcasted_iota(jnp.int32, (nb, tk), 1)
        expand = jnp.where(er == j * bpt + ec // blk, 1.0, 0.0).astype(BF16)
        keep = jnp.dot(sel, expand, preferred_element_type=F32) > 0.5
        if diag:
            kpos = j * tk + lax.broadcasted_iota(jnp.int32, (tq, tk), 1)
            keep = keep & (kpos <= qpos)
        keep = jnp.concatenate([keep] * rep, axis=0)
        s = jnp.where(keep, _dot_t(q4, k), NEG)
        _softmax_step(s, m_ref, l_ref, acc_ref, v)

    _softmax_reset(m_ref, l_ref, acc_ref)

    def sel_body(j, c):
        sel_tile(j, False)
        return c

    lax.fori_loop(0, n_full, sel_body, 0)
    sel_tile(n_full, True)
    o_sel = _softmax_result(l_ref, acc_ref)

    start = pl.multiple_of(jnp.maximum((i + 1) * tq - wspan, 0), tq)
    kw = kw_ref[pl.ds(start, wspan), :].astype(BF16)
    vw = vw_ref[pl.ds(start, wspan), :].astype(BF16)
    kpos = start + lax.broadcasted_iota(jnp.int32, (tq, wspan), 1)
    keep = (kpos <= qpos) & (kpos > qpos - cfg.window)
    keep = jnp.concatenate([keep] * rep, axis=0)
    _softmax_reset(m_ref, l_ref, acc_ref)
    _softmax_step(jnp.where(keep, _dot_t(q4, kw), NEG), m_ref, l_ref, acc_ref, vw)
    o_win = _softmax_result(l_ref, acc_ref)

    gates = jax.nn.sigmoid(gate_ref[...])
    for r in range(rep):
        h = gidx * rep + r
        lane = lax.broadcasted_iota(jnp.int32, gates.shape, 1)

        def gcol(c):
            return jnp.sum(jnp.where(lane == 3 * h + c, gates, 0.0), axis=-1, keepdims=True)

        o_ref[:, r * W:(r + 1) * W] = (gcol(0) * o_cmp[r] + gcol(1) * o_sel[r * tq:(r + 1) * tq]
                                       + gcol(2) * o_win[r * tq:(r + 1) * tq])


def nsa_prompt(cfg, bq_raw, bq_rot, cmp_kv, nsa_kv, win_kv, gates, bsz, n_t, tq=256, tk=512):
    tq = _pick(n_t, tq)
    tk = _pick(n_t, tk)
    assert tq % cfg.cmp_block == 0 and tk % tq == 0 and cfg.window % tq == 0
    wspan = min(cfg.window + tq, n_t)
    nq = n_t // tq
    g, W = cfg.b_kv, LANES
    rep = cfg.b_heads // g
    nb = n_t // cfg.cmp_block
    qspec = pl.BlockSpec((tq, rep * W), lambda b, gi, i: (b * nq + i, gi))

    def kvspec(off):
        return pl.BlockSpec((n_t, W), lambda b, gi, i: (b, off + gi))

    return pl.pallas_call(
        functools.partial(_nsa_prompt_kernel, cfg=cfg, tq=tq, tk=tk, wspan=wspan, rep=rep, nb=nb),
        grid=(bsz, g, nq),
        in_specs=[qspec, qspec,
                  pl.BlockSpec((nb, 2 * g * W), lambda b, gi, i: (b, 0)),
                  kvspec(2 * g), kvspec(3 * g), kvspec(0), kvspec(g),
                  pl.BlockSpec((tq, W), lambda b, gi, i: (b * nq + i, 0))],
        out_specs=qspec,
        out_shape=jax.ShapeDtypeStruct((bsz * n_t, cfg.b_heads * W), F32),
        scratch_shapes=[pltpu.VMEM((rep * tq, W), F32)] * 3,
        compiler_params=_cp(("parallel", "parallel", "arbitrary")),
        name="nsa_prompt",
    )(bq_raw, bq_rot, cmp_kv, nsa_kv, nsa_kv, win_kv, win_kv, gates)


def _softplus(z):
    return jnp.maximum(z, 0.0) + jnp.log1p(jnp.exp(-jnp.abs(z)))


def _sb_tile(q, k, v, upper, carry, mask):
    z = _dot_t(q, k)
    sp = _softplus(z)
    lm = -sp if mask is None else jnp.where(mask, -sp, 0.0)
    hi = lm.astype(BF16)
    lo = (lm - hi.astype(F32)).astype(BF16)
    between = (jnp.dot(hi, upper, preferred_element_type=F32)
               + jnp.dot(lo, upper, preferred_element_type=F32)) + carry
    w = jnp.exp(z - sp + between)
    if mask is not None:
        w = jnp.where(mask, w, 0.0)
    out = jnp.dot(w.astype(BF16), v, preferred_element_type=F32)
    return out, carry + jnp.sum(lm, axis=-1, keepdims=True)


def _sb_prompt_kernel(q_ref, k_ref, v_ref, o_ref, c_ref, acc_ref, *, tq):
    i = pl.program_id(2)
    q = (q_ref[...] * (LANES ** -0.5)).astype(BF16)
    r = lax.broadcasted_iota(jnp.int32, (tq, tq), 0)
    c = lax.broadcasted_iota(jnp.int32, (tq, tq), 1)
    upper = jnp.where(r > c, 1.0, 0.0).astype(BF16)
    k = k_ref[pl.ds(i * tq, tq), :].astype(BF16)
    v = v_ref[pl.ds(i * tq, tq), :].astype(BF16)
    out, carry = _sb_tile(q, k, v, upper, jnp.zeros((tq, 1), F32), c < r)
    acc_ref[...] = out
    c_ref[...] = carry

    def body(st):
        t, _ = st
        j = i - 1 - t
        kk = k_ref[pl.ds(j * tq, tq), :].astype(BF16)
        vv = v_ref[pl.ds(j * tq, tq), :].astype(BF16)
        o, cnew = _sb_tile(q, kk, vv, upper, c_ref[...], None)
        acc_ref[...] += o
        c_ref[...] = cnew
        return t + 1, jnp.max(cnew)

    lax.while_loop(lambda st: (st[0] < i) & (st[1] > SB_EXIT), body, (jnp.int32(0), jnp.max(carry)))
    o_ref[...] = acc_ref[...]


def sb_prompt(cfg, q, kv, bsz, n_t, tq=256):
    tq = _pick(n_t, tq)
    nq = n_t // tq
    ch, W = cfg.c_heads, LANES
    return pl.pallas_call(
        functools.partial(_sb_prompt_kernel, tq=tq),
        grid=(bsz, ch, nq),
        in_specs=[pl.BlockSpec((tq, W), lambda b, h, i: (b * nq + i, h)),
                  pl.BlockSpec((n_t, W), lambda b, h, i: (b, h)),
                  pl.BlockSpec((n_t, W), lambda b, h, i: (b, ch + h))],
        out_specs=pl.BlockSpec((tq, W), lambda b, h, i: (b * nq + i, h)),
        out_shape=jax.ShapeDtypeStruct((bsz * n_t, ch * W), F32),
        scratch_shapes=[pltpu.VMEM((tq, 1), F32), pltpu.VMEM((tq, W), F32)],
        compiler_params=_cp(("parallel", "parallel", "arbitrary")),
        name="sb_prompt",
    )(q, kv, kv)


def _route_kernel(h_ref, w_ref, id_ref, gate_ref, *, n_exp, top_k):
    logits = jnp.dot(h_ref[...], w_ref[...], preferred_element_type=F32, precision=lax.Precision.HIGHEST)
    idx = lax.broadcasted_iota(jnp.int32, logits.shape, 1)
    work = jnp.where(idx < n_exp, logits, -jnp.inf)
    ids = jnp.zeros(logits.shape, jnp.int32)
    ex = jnp.zeros(logits.shape, F32)
    top = None
    for t in range(top_k):
        mx = jnp.max(work, axis=-1, keepdims=True)
        first = jnp.min(jnp.where(work == mx, idx, logits.shape[-1]), axis=-1, keepdims=True)
        top = mx if top is None else top
        ids = jnp.where(idx == t, first, ids)
        ex = jnp.where(idx == t, jnp.exp(mx - top), ex)
        work = jnp.where(idx == first, -jnp.inf, work)
    id_ref[...] = ids
    gate_ref[...] = ex / jnp.sum(ex, axis=-1, keepdims=True)


def route(cfg, h, w_router_pad, tm=512):
    m, d = h.shape
    tm = _pick(m, tm)
    return pl.pallas_call(
        functools.partial(_route_kernel, n_exp=cfg.n_exp, top_k=cfg.top_k),
        grid=(m // tm,),
        in_specs=[pl.BlockSpec((tm, d), lambda i: (i, 0)),
                  pl.BlockSpec((d, LANES), lambda i: (0, 0))],
        out_specs=[pl.BlockSpec((tm, LANES), lambda i: (i, 0))] * 2,
        out_shape=[jax.ShapeDtypeStruct((m, LANES), jnp.int32), jax.ShapeDtypeStruct((m, LANES), F32)],
        compiler_params=_cp(("parallel",)),
        name="route",
    )(h, w_router_pad)


def _group_swiglu_kernel(te_ref, nu_ref, x_ref, wg_ref, wu_ref, o_ref):
    del te_ref
    t = pl.program_id(1)

    @pl.when(t < nu_ref[0])
    def _():
        _swiglu_kernel(x_ref, wg_ref, wu_ref, o_ref)

    @pl.when(t >= nu_ref[0])
    def _():
        o_ref[...] = jnp.zeros(o_ref.shape, o_ref.dtype)


def group_swiglu(x, w_gu, layer_i, tile_expert, n_used, d_ff, tm, tn=512):
    r, kdim = x.shape
    tn = _pick(d_ff, tn)
    nb = d_ff // tn
    wblk = (None, None, kdim, tn)
    return pl.pallas_call(
        _group_swiglu_kernel,
        grid_spec=pltpu.PrefetchScalarGridSpec(
            num_scalar_prefetch=2,
            grid=(nb, r // tm),
            in_specs=[pl.BlockSpec((tm, kdim), lambda j, t, te, nu: (t, 0)),
                      pl.BlockSpec(wblk, lambda j, t, te, nu: (layer_i, te[t], 0, j)),
                      pl.BlockSpec(wblk, lambda j, t, te, nu: (layer_i, te[t], 0, nb + j))],
            out_specs=pl.BlockSpec((tm, tn), lambda j, t, te, nu: (t, j))),
        out_shape=jax.ShapeDtypeStruct((r, d_ff), BF16),
        compiler_params=_cp(("arbitrary", "arbitrary")),
        name="group_swiglu",
    )(tile_expert, n_used, x, w_gu, w_gu)


def _group_down_kernel(te_ref, nu_ref, a_ref, w_ref, g_ref, hi_ref, lo_ref):
    del te_ref
    t = pl.program_id(1)

    @pl.when(t < nu_ref[0])
    def _():
        y = g_ref[...] * jnp.dot(a_ref[...], w_ref[...].astype(BF16), preferred_element_type=F32)
        hi = y.astype(BF16)
        hi_ref[...] = hi
        lo_ref[...] = (y - hi.astype(F32)).astype(BF16)

    @pl.when(t >= nu_ref[0])
    def _():
        hi_ref[...] = jnp.zeros(hi_ref.shape, hi_ref.dtype)
        lo_ref[...] = jnp.zeros(lo_ref.shape, lo_ref.dtype)


def group_down(a, w_down, layer_i, tile_expert, n_used, row_gate, tm, tn=512):
    r, kdim = a.shape
    n = w_down.shape[-1]
    tn = _pick(n, tn)
    out_spec = pl.BlockSpec((tm, tn), lambda j, t, te, nu: (t, j))
    return pl.pallas_call(
        _group_down_kernel,
        grid_spec=pltpu.PrefetchScalarGridSpec(
            num_scalar_prefetch=2,
            grid=(n // tn, r // tm),
            in_specs=[pl.BlockSpec((tm, kdim), lambda j, t, te, nu: (t, 0)),
                      pl.BlockSpec((None, None, kdim, tn), lambda j, t, te, nu: (layer_i, te[t], 0, j)),
                      pl.BlockSpec((tm, 1), lambda j, t, te, nu: (t, 0))],
            out_specs=[out_spec, out_spec]),
        out_shape=[jax.ShapeDtypeStruct((r, n), BF16)] * 2,
        compiler_params=_cp(("arbitrary", "arbitrary")),
        name="group_down",
    )(tile_expert, n_used, a, w_down, row_gate)


def _dispatch_kernel(c0_ref, nc_ref, src_ref, xp_ref, xs_ref, o_ref, acc_ref, *, ck, mp, n_ck):
    t, c = pl.program_id(0), pl.program_id(1)
    src = src_ref[...]

    def onehot(base, width):
        return jnp.where(src == base + lax.broadcasted_iota(jnp.int32, (1, width), 1), 1.0, 0.0).astype(BF16)

    @pl.when(c == 0)
    def _():
        acc_ref[...] = jnp.dot(onehot(mp, xs_ref.shape[0]), xs_ref[...].astype(BF16), preferred_element_type=F32)

    @pl.when(c < nc_ref[t])
    def _():
        acc_ref[...] += jnp.dot(onehot((c0_ref[t] + c) * ck, ck), xp_ref[...].astype(BF16),
                                preferred_element_type=F32)

    @pl.when(c == n_ck - 1)
    def _():
        o_ref[...] = acc_ref[...].astype(o_ref.dtype)


def dispatch_rows(xp, xs_pad, src_row, chunk0, n_chunk, tm, ck=512):
    mp, d = xp.shape
    r = src_row.shape[0]
    ck = _pick(mp, ck)
    n_ck = mp // ck
    return pl.pallas_call(
        functools.partial(_dispatch_kernel, ck=ck, mp=mp, n_ck=n_ck),
        grid_spec=pltpu.PrefetchScalarGridSpec(
            num_scalar_prefetch=2,
            grid=(r // tm, n_ck),
            in_specs=[pl.BlockSpec((tm, 1), lambda t, c, c0, nc: (t, 0)),
                      pl.BlockSpec((ck, d), lambda t, c, c0, nc:
                                   (c0[t] + jnp.minimum(c, jnp.maximum(nc[t] - 1, 0)), 0)),
                      pl.BlockSpec(xs_pad.shape, lambda t, c, c0, nc: (0, 0))],
            out_specs=pl.BlockSpec((tm, d), lambda t, c, c0, nc: (t, 0)),
            scratch_shapes=[pltpu.VMEM((tm, d), F32)]),
        out_shape=jax.ShapeDtypeStruct((r, d), BF16),
        compiler_params=_cp(("arbitrary", "arbitrary")),
        name="dispatch_rows",
    )(chunk0, n_chunk, src_row[:, None], xp, xs_pad)


def _combine_kernel(clo_ref, chi_ref, d_ref, x_ref, hi_ref, lo_ref, g_ref, b_ref, o_ref, acc_ref,
                    *, ck, n_exp, n_c, alpha, eps):
    tt, e, c = pl.program_id(0), pl.program_id(1), pl.program_id(2)

    @pl.when((e == 0) & (c == 0))
    def _():
        acc_ref[...] = jnp.zeros(acc_ref.shape, F32)

    chunk = clo_ref[tt * n_exp + e] + c

    @pl.when(chunk <= chi_ref[tt * n_exp + e])
    def _():
        rows = chunk * ck + lax.broadcasted_iota(jnp.int32, (1, ck), 1)
        dd = d_ref[...]
        p = jnp.where((dd[:, 0:1] == rows) | (dd[:, 1:2] == rows), 1.0, 0.0).astype(BF16)
        acc_ref[...] += (jnp.dot(p, hi_ref[...], preferred_element_type=F32)
                         + jnp.dot(p, lo_ref[...], preferred_element_type=F32))

    @pl.when((e == n_exp - 1) & (c == n_c - 1))
    def _():
        y = alpha * x_ref[...] + acc_ref[...]
        mu = jnp.mean(y, axis=-1, keepdims=True)
        dv = y - mu
        var = jnp.mean(dv * dv, axis=-1, keepdims=True)
        o_ref[...] = dv * lax.rsqrt(var + eps) * g_ref[...] + b_ref[...]


def combine_ln(cfg, x, dest, y_hi, y_lo, clo, chi, g, b, layer, tm, ck, n_c):
    m, d = x.shape
    n_exp = cfg.n_exp

    def ymap(tt, e, c, lo, hi):
        s = tt * n_exp + e
        return (jnp.maximum(jnp.minimum(lo[s] + c, hi[s]), 0), 0)

    return pl.pallas_call(
        functools.partial(_combine_kernel, ck=ck, n_exp=n_exp, n_c=n_c, alpha=cfg.alpha, eps=cfg.ln_eps),
        grid_spec=pltpu.PrefetchScalarGridSpec(
            num_scalar_prefetch=2,
            grid=(m // tm, n_exp, n_c),
            in_specs=[pl.BlockSpec((tm, 2), lambda tt, e, c, lo, hi: (tt, 0)),
                      pl.BlockSpec((tm, d), lambda tt, e, c, lo, hi: (tt, 0)),
                      pl.BlockSpec((ck, d), ymap), pl.BlockSpec((ck, d), ymap),
                      pl.BlockSpec((None, 1, d), lambda tt, e, c, lo, hi: (layer, 0, 0)),
                      pl.BlockSpec((None, 1, d), lambda tt, e, c, lo, hi: (layer, 0, 0))],
            out_specs=pl.BlockSpec((tm, d), lambda tt, e, c, lo, hi: (tt, 0)),
            scratch_shapes=[pltpu.VMEM((tm, d), F32)]),
        out_shape=jax.ShapeDtypeStruct((m, d), F32),
        compiler_params=_cp(("arbitrary", "arbitrary", "arbitrary")),
        name="combine_ln",
    )(clo, chi, dest, x, y_hi, y_lo, g.reshape(-1, 1, d), b.reshape(-1, 1, d))


def moe_layer(cfg, xp, xs, moe_router, moe_w_gu, moe_w_down, layer_i, ln_g, ln_b, layer, tm=512):
    mp, d = xp.shape
    ms = xs.shape[0]
    m = mp + ms
    k, n_exp = cfg.top_k, cfg.n_exp
    assert k == 2
    w_r = jnp.pad(moe_router[layer_i], ((0, 0), (0, LANES - n_exp)))
    ids_p, gts_p = route(cfg, xp, w_r)
    ids_s, gts_s = route(cfg, xs, w_r)
    ids = jnp.concatenate([ids_p[:, :k], ids_s[:, :k]], axis=0).reshape(-1)
    gts = jnp.concatenate([gts_p[:, :k], gts_s[:, :k]], axis=0).reshape(-1)
    n_asg = m * k
    n_tiles = (n_asg + n_exp * (tm - 1)) // tm
    order = jnp.argsort(ids, stable=True).astype(jnp.int32)
    e_sorted = ids[order]
    counts = jnp.sum(ids[:, None] == jnp.arange(n_exp, dtype=jnp.int32)[None, :], axis=0).astype(jnp.int32)
    tiles_per = (counts + tm - 1) // tm
    tile_end = jnp.cumsum(tiles_per)
    pad_start = (tile_end - tiles_per) * tm
    grp_start = jnp.cumsum(counts) - counts
    dest_sorted = pad_start[e_sorted] + jnp.arange(n_asg, dtype=jnp.int32) - grp_start[e_sorted]
    n_used = tile_end[-1:].astype(jnp.int32)
    tile_ids = jnp.minimum(jnp.arange(n_tiles, dtype=jnp.int32), n_used[0] - 1)
    tile_expert = jnp.sum(tile_ids[:, None] >= tile_end[None, :], axis=1).astype(jnp.int32)
    rows = jnp.arange(n_tiles * tm, dtype=jnp.int32)
    row_e = tile_expert[rows // tm]
    rank = rows - pad_start[row_e]
    real = (rank < counts[row_e]) & (rows // tm < n_used[0])
    asg = order[jnp.clip(grp_start[row_e] + rank, 0, n_asg - 1)]
    src_row = jnp.where(real, asg // k, -1)
    row_gate = jnp.where(real, gts[asg], 0.0)
    dest = jnp.zeros((n_asg,), jnp.int32).at[order].set(dest_sorted).reshape(m, k)
    ck = _pick(mp, 512)
    big = mp // ck
    in_p = (src_row >= 0) & (src_row < mp)
    src_ck = (src_row // ck).reshape(n_tiles, tm)
    c_min = jnp.min(jnp.where(in_p.reshape(n_tiles, tm), src_ck, big), axis=1)
    c_max = jnp.max(jnp.where(in_p.reshape(n_tiles, tm), src_ck, -1), axis=1)
    n_chunk = jnp.maximum(c_max - c_min + 1, 0).astype(jnp.int32)
    chunk0 = jnp.where(n_chunk > 0, c_min, 0).astype(jnp.int32)
    xs_pad = jnp.pad(xs, ((0, LANES - ms), (0, 0)))
    x_sorted = dispatch_rows(xp, xs_pad, src_row, chunk0, n_chunk, tm, ck)
    a = group_swiglu(x_sorted, moe_w_gu, layer_i, tile_expert, n_used, cfg.d_ff, tm)
    y_hi, y_lo = group_down(a, moe_w_down, layer_i, tile_expert, n_used, row_gate[:, None], tm)
    tmt = _pick(mp, tm)
    ck2 = tm // 2
    n_c = tmt // ck2 + 1
    per_tok = jnp.sum(ids.reshape(m, k)[:, :, None] == jnp.arange(n_exp, dtype=jnp.int32)[None, None, :], axis=1)
    before = jnp.concatenate([jnp.zeros((1, n_exp), jnp.int32), jnp.cumsum(per_tok, axis=0).astype(jnp.int32)])

    def chunk_range(t0, t1):
        lo_r, hi_r = before[t0], before[t1] - 1
        lo = (pad_start + lo_r) // ck2
        hi = (pad_start + hi_r) // ck2
        empty = hi_r < lo_r
        return jnp.where(empty, 1, lo).astype(jnp.int32), jnp.where(empty, 0, hi).astype(jnp.int32)

    edges = jnp.arange(0, mp + 1, tmt, dtype=jnp.int32)
    clo_p, chi_p = chunk_range(edges[:-1], edges[1:])
    clo_s, chi_s = chunk_range(jnp.array([mp]), jnp.array([m]))
    assert ms <= ck2
    xp_new = combine_ln(cfg, xp, dest[:mp], y_hi, y_lo, clo_p.reshape(-1), chi_p.reshape(-1),
                        ln_g, ln_b, layer, tmt, ck2, n_c)
    xs_new = combine_ln(cfg, xs, dest[mp:], y_hi, y_lo, clo_s.reshape(-1), chi_s.reshape(-1),
                        ln_g, ln_b, layer, ms, ck2, n_c)
    return xp_new, xs_new


PAGES_PER_STEP = 4


def _diff_sample_kernel(pt_ref, q_ref, new_ref, *rest, d_qk, lam_init, eps, n_steps, n_pp):
    del pt_ref
    page_refs = rest[:n_pp]
    lam_ref, g_ref, o_ref, m_ref, l_ref, acc_ref = rest[n_pp:]
    j = pl.program_id(1)
    q = q_ref[...] * (d_qk ** -0.5)
    lo = lax.broadcasted_iota(jnp.int32, q.shape, 1) < d_qk

    def scores(k):
        prod = k * q[None]
        s1 = jnp.sum(jnp.where(lo[None], prod, 0.0), axis=-1, keepdims=True)
        s2 = jnp.sum(jnp.where(lo[None], 0.0, prod), axis=-1, keepdims=True)
        return s1, s2

    @pl.when(j == 0)
    def _():
        s1, s2 = scores(new_ref[0][None])
        m_ref[0] = s1[0]
        m_ref[1] = s2[0]
        l_ref[...] = jnp.ones(l_ref.shape, F32)
        acc_ref[0] = new_ref[1]
        acc_ref[1] = new_ref[1]

    for page_ref in page_refs:
        v = page_ref[:, 1]
        for t, s in enumerate(scores(page_ref[:, 0])):
            m_old = m_ref[t]
            m_new = jnp.maximum(m_old, jnp.max(s, axis=0))
            a = jnp.exp(m_old - m_new)
            p = jnp.exp(s - m_new[None])
            l_ref[t] = a * l_ref[t] + jnp.sum(p, axis=0)
            acc_ref[t] = a * acc_ref[t] + jnp.sum(p * v, axis=0)
            m_ref[t] = m_new

    @pl.when(j == n_steps - 1)
    def _():
        lam = _diff_lambda(lam_ref, lam_init)
        o_ref[...] = _diff_finish(acc_ref[0] / l_ref[0], acc_ref[1] / l_ref[1], lam, g_ref[...], lam_init, eps)


def diff_sample(cfg, aq, dkv, cache, page_table, lam_vec, subln_g, layer_i, lam_init):
    dbsz = aq.shape[0]
    n_pages = page_table.shape[1]
    ah, W = cfg.a_heads, LANES
    n_pp = math.gcd(PAGES_PER_STEP, n_pages)
    n_steps = n_pages // n_pp

    def page_spec(u):
        return pl.BlockSpec((None, None, cfg.page, 2, ah, W),
                            lambda b, j, pt: (layer_i, pt[b, j * n_pp + u], 0, 0, 0, 0))

    out = pl.pallas_call(
        functools.partial(_diff_sample_kernel, d_qk=cfg.a_qk, lam_init=lam_init, eps=cfg.rms_eps,
                          n_steps=n_steps, n_pp=n_pp),
        grid_spec=pltpu.PrefetchScalarGridSpec(
            num_scalar_prefetch=1,
            grid=(dbsz, n_steps),
            in_specs=[pl.BlockSpec((None, ah, W), lambda b, j, pt: (b, 0, 0)),
                      pl.BlockSpec((None, 2, ah, W), lambda b, j, pt: (b, 0, 0, 0))]
                     + [page_spec(u) for u in range(n_pp)]
                     + [pl.BlockSpec((None, 4, cfg.a_qk), lambda b, j, pt: (layer_i, 0, 0)),
                        pl.BlockSpec((None, 1, W), lambda b, j, pt: (layer_i, 0, 0))],
            out_specs=pl.BlockSpec((None, ah, W), lambda b, j, pt: (b, 0, 0)),
            scratch_shapes=[pltpu.VMEM((2, ah, 1), F32), pltpu.VMEM((2, ah, 1), F32), pltpu.VMEM((2, ah, W), F32)]),
        out_shape=jax.ShapeDtypeStruct((dbsz, ah, W), F32),
        compiler_params=_cp(("parallel", "arbitrary")),
        name="diff_sample",
    )(page_table, aq.reshape(dbsz, ah, W), dkv.reshape(dbsz, 2, ah, W), *([cache] * n_pp), lam_vec,
      subln_g.reshape(-1, 1, W))
    return out.reshape(dbsz, ah * W)


def _suffix_sum(x):
    n = x.shape[0]
    sh = 1
    while sh < n:
        x = x + jnp.concatenate([x[sh:], jnp.zeros((sh,) + x.shape[1:], x.dtype)], axis=0)
        sh *= 2
    return x


def _sb_sample_kernel(pt_ref, q_ref, c0_ref, a0_ref, page_ref, o_ref, cout_ref, c_ref, acc_ref, *, n_pages):
    del pt_ref
    j = pl.program_id(1)

    @pl.when(j == 0)
    def _():
        c_ref[...] = c0_ref[...]
        acc_ref[...] = a0_ref[...]

    q = q_ref[...] * (LANES ** -0.5)
    z = jnp.sum(page_ref[:, 0] * q[None], axis=-1, keepdims=True)
    sp = _softplus(z)
    inc = _suffix_sum(-sp)
    between = inc + sp + c_ref[...][None]
    w = jnp.exp(z - sp + between)
    acc_ref[...] += jnp.sum(w * page_ref[:, 1], axis=0)
    c_ref[...] += inc[0]

    @pl.when(j == n_pages - 1)
    def _():
        o_ref[...] = acc_ref[...]
        cout_ref[...] = c_ref[...]


SB_TAIL_PAGES = 4


def _sb_sample_pages(cfg, q3, cache, page_table, layer_i, first_page, n_walk, c0, a0):
    dbsz, ch, W = q3.shape
    vec = pl.BlockSpec((None, ch, W), lambda b, j, pt: (b, 0, 0))
    one = pl.BlockSpec((None, ch, 1), lambda b, j, pt: (b, 0, 0))
    return pl.pallas_call(
        functools.partial(_sb_sample_kernel, n_pages=n_walk),
        grid_spec=pltpu.PrefetchScalarGridSpec(
            num_scalar_prefetch=1,
            grid=(dbsz, n_walk),
            in_specs=[vec, one, vec,
                      pl.BlockSpec((None, None, cfg.page, 2, ch, W),
                                   lambda b, j, pt: (layer_i, pt[b, first_page - j], 0, 0, 0, 0))],
            out_specs=[vec, one],
            scratch_shapes=[pltpu.VMEM((ch, 1), F32), pltpu.VMEM((ch, W), F32)]),
        out_shape=[jax.ShapeDtypeStruct((dbsz, ch, W), F32), jax.ShapeDtypeStruct((dbsz, ch, 1), F32)],
        compiler_params=_cp(("parallel", "arbitrary")),
        name="sb_sample",
    )(page_table, q3, c0, a0, cache)


def sb_sample(cfg, q, cache, page_table, layer_i):
    dbsz = q.shape[0]
    n_pages = page_table.shape[1]
    ch, W = cfg.c_heads, LANES
    q3 = q.reshape(dbsz, ch, W)
    tail = min(SB_TAIL_PAGES, n_pages)
    out, carry = _sb_sample_pages(cfg, q3, cache, page_table, layer_i, n_pages - 1, tail,
                                  jnp.zeros((dbsz, ch, 1), F32), jnp.zeros((dbsz, ch, W), F32))
    if n_pages > tail:
        out = lax.cond(jnp.max(carry) > SB_EXIT,
                       lambda: _sb_sample_pages(cfg, q3, cache, page_table, layer_i, n_pages - 1 - tail,
                                                n_pages - tail, carry, out)[0],
                       lambda: out)
    return out.reshape(dbsz, ch * W)


def _nsa_cmp_sample_kernel(pt_ref, q_ref, *rest, cfg, n_steps, n_pp, nb):
    del pt_ref
    page_refs = rest[:n_pp]
    pe_ref, w_ref, o_ref, sel_ref, pool_ref = rest[n_pp:]
    j = pl.program_id(1)
    blk, g, W = cfg.cmp_block, cfg.b_kv, LANES
    rep = cfg.b_heads // g
    per_page = cfg.page // blk
    for u, page_ref in enumerate(page_refs):
        x = page_ref[...]
        for t in range(per_page):
            s = jnp.sum(x[t * blk:(t + 1) * blk], axis=0) * (1.0 / blk)
            for sl in range(2):
                for gg in range(g):
                    pool_ref[sl, gg, pl.ds((j * n_pp + u) * per_page + t, 1), :] = s[sl, gg:gg + 1, :]

    @pl.when(j == n_steps - 1)
    def _():
        scale = W ** -0.5
        bidx = lax.broadcasted_iota(jnp.int32, (1, nb), 1)
        lane = lax.broadcasted_iota(jnp.int32, (1, W), 1)
        for gg in range(g):
            kv = []
            for sl in range(2):
                pe = jnp.mean(pe_ref[sl], axis=0, keepdims=True)
                kv.append(jnp.dot((pool_ref[sl, gg] + pe).astype(BF16), w_ref[sl].astype(BF16),
                                  preferred_element_type=F32))
            q = q_ref[gg * rep:(gg + 1) * rep, :]
            sc = _dot_t(q.astype(BF16), kv[0].astype(BF16)) * scale
            e = jnp.exp(sc - jnp.max(sc, axis=-1, keepdims=True))
            p = e / jnp.sum(e, axis=-1, keepdims=True)
            o_ref[gg * rep:(gg + 1) * rep, :] = jnp.dot(p.astype(BF16), kv[1].astype(BF16),
                                                        preferred_element_type=F32)
            imp = jnp.sum(p, axis=0, keepdims=True)
            forced = (bidx == 0) | (bidx > nb - cfg.n_local)
            work = jnp.where(forced, cfg.forced, imp)
            picks = jnp.zeros((1, W), jnp.int32)
            for t in range(cfg.top_n - 1):
                mx = jnp.max(work, axis=-1, keepdims=True)
                first = jnp.min(jnp.where(work == mx, bidx, nb), axis=-1, keepdims=True)
                picks = jnp.where(lane == t, first, picks)
                work = jnp.where(bidx == first, -jnp.inf, work)
            sel_ref[gg:gg + 1, :] = picks


def nsa_cmp_sample(cfg, bq_raw, cache, page_table, cmp_pe, cmp_w, layer_i):
    dbsz = bq_raw.shape[0]
    n_pages = page_table.shape[1]
    g, W, blk = cfg.b_kv, LANES, cfg.cmp_block
    nb = n_pages * cfg.page // blk
    assert nb >= cfg.top_n and cfg.page % blk == 0
    n_pp = math.gcd(2 * PAGES_PER_STEP, n_pages)
    n_steps = n_pages // n_pp

    def page_spec(u):
        return pl.BlockSpec((None, None, cfg.page, 2, g, W),
                            lambda b, j, pt: (layer_i, pt[b, j * n_pp + u], 0, 0, 0, 0))

    return pl.pallas_call(
        functools.partial(_nsa_cmp_sample_kernel, cfg=cfg, n_steps=n_steps, n_pp=n_pp, nb=nb),
        grid_spec=pltpu.PrefetchScalarGridSpec(
            num_scalar_prefetch=1,
            grid=(dbsz, n_steps),
            in_specs=[pl.BlockSpec((None, cfg.b_heads, W), lambda b, j, pt: (b, 0, 0))]
                     + [page_spec(u) for u in range(n_pp)]
                     + [pl.BlockSpec((None, 2, blk, W), lambda b, j, pt: (layer_i, 0, 0, 0)),
                        pl.BlockSpec((None, 2, W, W), lambda b, j, pt: (layer_i, 0, 0, 0))],
            out_specs=[pl.BlockSpec((None, cfg.b_heads, W), lambda b, j, pt: (b, 0, 0)),
                       pl.BlockSpec((None, g, W), lambda b, j, pt: (b, 0, 0))],
            scratch_shapes=[pltpu.VMEM((2, g, nb, W), F32)]),
        out_shape=[jax.ShapeDtypeStruct((dbsz, cfg.b_heads, W), F32),
                   jax.ShapeDtypeStruct((dbsz, g, W), jnp.int32)],
        compiler_params=_cp(("parallel", "arbitrary")),
        name="nsa_cmp_sample",
    )(page_table, bq_raw.reshape(dbsz, cfg.b_heads, W), *([cache] * n_pp), cmp_pe, cmp_w)


def _nsa_sel_sample_kernel(pt_ref, sel_ref, q_ref, new_ref, blk0_ref, blk1_ref, win_ref, wnew_ref,
                           osel_ref, owin_ref, m_ref, l_ref, acc_ref, *, cfg, n_steps, keep):
    del pt_ref, sel_ref
    s_idx = pl.program_id(1)
    W = LANES
    rep = cfg.b_heads // cfg.b_kv
    scale = W ** -0.5

    @pl.when(s_idx == 0)
    def _():
        for r in range(rep):
            sc = jnp.sum(new_ref[2] * q_ref[r], axis=-1, keepdims=True) * scale
            m_ref[r] = sc
            l_ref[r] = jnp.ones(sc.shape, F32)
            acc_ref[r] = new_ref[3]

    sub = lax.broadcasted_iota(jnp.int32, blk0_ref.shape[:1] + blk0_ref.shape[2:], 1)
    kk = jnp.where(sub == 0, blk0_ref[:, 0], blk1_ref[:, 0])
    vv = jnp.where(sub == 0, blk0_ref[:, 1], blk1_ref[:, 1])
    for r in range(rep):
        s = jnp.sum(kk * q_ref[r][None], axis=-1, keepdims=True) * scale
        m_old = m_ref[r]
        m_new = jnp.maximum(m_old, jnp.max(s, axis=0))
        a = jnp.exp(m_old - m_new)
        p = jnp.exp(s - m_new[None])
        l_ref[r] = a * l_ref[r] + jnp.sum(p, axis=0)
        acc_ref[r] = a * acc_ref[r] + jnp.sum(p * vv, axis=0)
        m_ref[r] = m_new

    @pl.when(s_idx == n_steps - 1)
    def _():
        kw, vw = win_ref[:, 0], win_ref[:, 1]
        rowi = lax.broadcasted_iota(jnp.int32, (keep, cfg.b_kv, 1), 0)
        valid = rowi > keep - cfg.window
        for r in range(rep):
            osel_ref[r] = acc_ref[r] / l_ref[r]
            q = q_ref[r]
            s = jnp.where(valid, jnp.sum(kw * q[None], axis=-1, keepdims=True) * scale, NEG)
            sn = jnp.sum(wnew_ref[0] * q, axis=-1, keepdims=True) * scale
            mx = jnp.maximum(jnp.max(s, axis=0), sn)
            p = jnp.exp(s - mx[None])
            pn = jnp.exp(sn - mx)
            owin_ref[r] = (jnp.sum(p * vw, axis=0) + pn * wnew_ref[1]) / (jnp.sum(p, axis=0) + pn)


def nsa_sel_sample(cfg, bq_rot, nkv, wkv, cache, win_state, page_table, sel_idx, layer_i):
    dbsz = bq_rot.shape[0]
    g, W, blk = cfg.b_kv, LANES, cfg.cmp_block
    rep = cfg.b_heads // g
    per_page = cfg.page // blk
    keep = win_state.shape[2]
    n_steps = cfg.top_n - 1
    q = jnp.transpose(bq_rot.reshape(dbsz, g, rep, W), (0, 2, 1, 3))

    def blk_spec(gg):
        def index(b, s, pt, sel):
            bid = sel[(b * g + gg) * W + s]
            return (layer_i, pt[b, bid // per_page], bid % per_page, 1, 0, 0)
        return pl.BlockSpec((None, None, blk, 2, g, W), index)

    assert g == 2
    out_spec = pl.BlockSpec((None, rep, g, W), lambda b, s, pt, sel: (b, 0, 0, 0))
    return pl.pallas_call(
        functools.partial(_nsa_sel_sample_kernel, cfg=cfg, n_steps=n_steps, keep=keep),
        grid_spec=pltpu.PrefetchScalarGridSpec(
            num_scalar_prefetch=2,
            grid=(dbsz, n_steps),
            in_specs=[pl.BlockSpec((None, rep, g, W), lambda b, s, pt, sel: (b, 0, 0, 0)),
                      pl.BlockSpec((None, 4, g, W), lambda b, s, pt, sel: (b, 0, 0, 0)),
                      blk_spec(0), blk_spec(1),
                      pl.BlockSpec((None, None, keep, 2, g, W), lambda b, s, pt, sel: (layer_i, b, 0, 0, 0, 0)),
                      pl.BlockSpec((None, 2, g, W), lambda b, s, pt, sel: (b, 0, 0, 0))],
            out_specs=[out_spec, out_spec],
            scratch_shapes=[pltpu.VMEM((rep, g, 1), F32), pltpu.VMEM((rep, g, 1), F32),
                            pltpu.VMEM((rep, g, W), F32)]),
        out_shape=[jax.ShapeDtypeStruct((dbsz, rep, g, W), F32)] * 2,
        compiler_params=_cp(("parallel", "arbitrary")),
        name="nsa_sel_sample",
    )(page_table, sel_idx.reshape(-1), q, nkv.reshape(dbsz, 4, g, W), cache, cache, win_state,
      wkv.reshape(dbsz, 2, g, W))


def _forward(cfg, x_prompt, x_sample, cache_diff_kv, cache_nsa_kv, state_nsa_win, cache_sb_kv, page_table,
             even_w_in, even_w_out, diff_lambda, diff_subln_g, nsa_cmp_pe, nsa_cmp_w,
             odd_w_in, odd_w_out, ln1_g, ln1_b, ln2_g, ln2_b,
             ffn_w_gu, ffn_w_down, moe_router, moe_w_gu, moe_w_down):
    bsz, n_t, d = x_prompt.shape
    dbsz = x_sample.shape[0]
    past_len = page_table.shape[1] * cfg.page
    W = LANES
    xp = x_prompt.reshape(bsz * n_t, d)
    xs = x_sample.reshape(dbsz, d)
    tabs_p = rope_tables(cfg, jnp.arange(n_t, dtype=jnp.int32))
    tabs_s = rope_tables(cfg, jnp.full((dbsz,), past_len, dtype=jnp.int32))
    outs = {k: [] for k in ("diff_p", "diff_s", "nsa_p", "nsa_s", "win_p", "win_s", "sb_p", "sb_s")}
    n_gate = 3 * cfg.b_heads
    for layer in range(cfg.depth):
        i = layer // 2
        if layer % 2 == 0:
            lam_init = 0.8 - 0.6 * math.exp(-0.3 * layer)
            w_gate = jnp.pad(even_w_in[i][:, cfg.even_main:], ((0, 0), (0, W - n_gate)))
            proj = matmul(xp, even_w_in, (i,), 0, cfg.even_main)
            gates = matmul(xp, w_gate)
            aq, dkv, bqr, bqt, nkv, wkv = even_split(cfg, proj, tabs_p, n_t)
            a_o = diff_prompt(cfg, aq, dkv, diff_lambda, diff_subln_g, i, lam_init, bsz, n_t)
            ckv = cmp_prep(cfg, nkv, nsa_cmp_pe, nsa_cmp_w, i)
            b_o = nsa_prompt(cfg, bqr, bqt, ckv, nkv, wkv, gates, bsz, n_t)
            mp = matmul_cat(a_o, b_o, even_w_out, (i,))
            keep = min(cfg.window, n_t)
            outs["diff_p"].append(dkv.reshape(bsz, n_t, 2, cfg.a_heads, W))
            outs["nsa_p"].append(nkv.reshape(bsz, n_t, 4, cfg.b_kv, W))
            outs["win_p"].append(wkv.reshape(bsz, n_t, 2, cfg.b_kv, W)[:, n_t - keep:])
            ms, dks, nks, wsn = _even_sample(cfg, xs, even_w_in, w_gate, even_w_out, i, tabs_s, lam_init,
                                             cache_diff_kv, cache_nsa_kv, state_nsa_win, page_table,
                                             diff_lambda, diff_subln_g, nsa_cmp_pe, nsa_cmp_w, past_len)
            outs["diff_s"].append(dks)
            outs["nsa_s"].append(nks)
            outs["win_s"].append(wsn)
        else:
            n_q = cfg.c_heads * W
            q = matmul(xp, odd_w_in, (i,), 0, n_q)
            kv = matmul(xp, odd_w_in, (i,), n_q, 2 * n_q)
            o = sb_prompt(cfg, q, kv, bsz, n_t)
            mp = matmul(o, odd_w_out, (i,))
            outs["sb_p"].append(kv.reshape(bsz, n_t, 2, cfg.c_heads, W))
            ms, sks = _odd_sample(cfg, xs, odd_w_in, odd_w_out, i, cache_sb_kv, page_table)
            outs["sb_s"].append(sks)
        xp = add_ln(cfg, xp, mp, ln1_g, ln1_b, layer)
        xs = add_ln(cfg, xs, ms, ln1_g, ln1_b, layer)
        if layer % 2 == 0:
            fp = matmul(swiglu_up(xp, ffn_w_gu, (i,), cfg.d_ff), ffn_w_down, (i,), tm=512, tn=512, w_outer=True)
            fs = matmul(swiglu_up(xs, ffn_w_gu, (i,), cfg.d_ff), ffn_w_down, (i,), tm=512, tn=512, w_outer=True)
            xp = add_ln(cfg, xp, fp, ln2_g, ln2_b, layer)
            xs = add_ln(cfg, xs, fs, ln2_g, ln2_b, layer)
        else:
            xp, xs = moe_layer(cfg, xp, xs, moe_router, moe_w_gu, moe_w_down, i, ln2_g, ln2_b, layer)
    st = {k: jnp.stack(v) for k, v in outs.items()}
    return (xp.reshape(bsz, n_t, d), xs.reshape(dbsz, 1, d), st["diff_p"], st["diff_s"], st["nsa_p"], st["nsa_s"],
            st["win_p"], st["win_s"], st["sb_p"], st["sb_s"])


def _even_sample(cfg, xs, even_w_in, w_gate, even_w_out, i, tabs_s, lam_init,
                 cache_diff_kv, cache_nsa_kv, state_nsa_win, page_table,
                 diff_lambda, diff_subln_g, nsa_cmp_pe, nsa_cmp_w, past_len):
    dbsz = xs.shape[0]
    W, g = LANES, cfg.b_kv
    rep = cfg.b_heads // g
    assert past_len % cfg.cmp_block == 0
    proj = matmul(xs, even_w_in, (i,), 0, cfg.even_main)
    gates = jax.nn.sigmoid(matmul(xs, w_gate)[:, :3 * cfg.b_heads]).reshape(dbsz, cfg.b_heads, 3)
    aq, dkv, bqr, bqt, nkv, wkv = even_split(cfg, proj, tabs_s, dbsz)
    a_o = diff_sample(cfg, aq, dkv, cache_diff_kv, page_table, diff_lambda, diff_subln_g, i, lam_init)
    o_cmp, sel_idx = nsa_cmp_sample(cfg, bqr, cache_nsa_kv, page_table, nsa_cmp_pe, nsa_cmp_w, i)
    o_sel, o_win = nsa_sel_sample(cfg, bqt, nkv, wkv, cache_nsa_kv, state_nsa_win, page_table, sel_idx, i)
    o_sel = jnp.transpose(o_sel, (0, 2, 1, 3)).reshape(dbsz, cfg.b_heads, W)
    o_win = jnp.transpose(o_win, (0, 2, 1, 3)).reshape(dbsz, cfg.b_heads, W)
    b_o = gates[..., 0:1] * o_cmp + gates[..., 1:2] * o_sel + gates[..., 2:3] * o_win
    ms = matmul(jnp.concatenate([a_o, b_o.reshape(dbsz, -1)], axis=-1), even_w_out, (i,))
    keep = state_nsa_win.shape[2]
    win = jnp.concatenate([state_nsa_win[i], wkv.reshape(dbsz, 1, 2, g, W)], axis=1)
    return (ms, dkv.reshape(dbsz, 1, 2, cfg.a_heads, W), nkv.reshape(dbsz, 1, 4, g, W), win[:, 1:keep + 1])


def _odd_sample(cfg, xs, odd_w_in, odd_w_out, i, cache_sb_kv, page_table):
    dbsz = xs.shape[0]
    ch, W = cfg.c_heads, LANES
    qkv = matmul(xs, odd_w_in, (i,))
    o = sb_sample(cfg, qkv[:, :ch * W], cache_sb_kv, page_table, i)
    return matmul(o, odd_w_out, (i,)), qkv[:, ch * W:].reshape(dbsz, 1, 2, ch, W)


def kernel(x_prompt, x_sample, cache_diff_kv, cache_nsa_kv, state_nsa_win, cache_sb_kv, page_table, even_w_in, even_w_out, diff_lambda, diff_subln_g, nsa_cmp_pe, nsa_cmp_w, odd_w_in, odd_w_out, ln1_g, ln1_b, ln2_g, ln2_b, ffn_w_gu, ffn_w_down, moe_router, moe_w_gu, moe_w_down):
    return _forward(Cfg(), x_prompt, x_sample, cache_diff_kv, cache_nsa_kv, state_nsa_win, cache_sb_kv, page_table,
                    even_w_in, even_w_out, diff_lambda, diff_subln_g, nsa_cmp_pe, nsa_cmp_w,
                    odd_w_in, odd_w_out, ln1_g, ln1_b, ln2_g, ln2_b,
                    ffn_w_gu, ffn_w_down, moe_router, moe_w_gu, moe_w_down)
```
